```python
import math
import jax, jax.numpy as jnp
from jax import lax
import numpy as np

D_MODEL = 2048
BATCH = 4
SEQ = 4096
DEPTH = 1

EPS = 1e-6
BLOCK = 128
NEG_INF = -1e30

MLA_HEADS = 8
MLA_NOPE = 128
MLA_ROPE = 64
MLA_V = 128
MLA_Q_RANK = 512
MLA_KV_RANK = 256
MLA_QK = MLA_NOPE + MLA_ROPE
ROPE_THETA = 10000.0

SWA_HEADS = 16
SWA_KV_HEADS = 2
SWA_HD = 64
SWA_GROUP = SWA_HEADS // SWA_KV_HEADS
WINDOW = 128

N_BUCKETS = 32
MAX_DISTANCE = 128

PEER_HEADS = 8
PEER_NKEYS = 128
PEER_N = PEER_NKEYS * PEER_NKEYS
PEER_DQ = 256
PEER_TOPK = 16
PEER_CHUNK = 64

MLA_OUT = MLA_HEADS * MLA_V
SWA_OUT = SWA_HEADS * SWA_HD
MIX_WIDTH = MLA_OUT + SWA_OUT
IN_SPLITS = (MLA_Q_RANK, MLA_KV_RANK, MLA_ROPE, SWA_HEADS * SWA_HD, SWA_KV_HEADS * SWA_HD, SWA_KV_HEADS * SWA_HD)
IN_COLS = sum(IN_SPLITS)

kernel_name = "hybrid_mla_swa_peer_block"


def rms_norm(x, g):
    xf = x.astype(jnp.float32)
    y = xf * lax.rsqrt(jnp.mean(xf * xf, axis=-1, keepdims=True) + EPS)
    return (y * g.astype(jnp.float32)).astype(x.dtype)


def rope(x, positions):
    half = x.shape[-1] // 2
    inv_freq = ROPE_THETA ** (-jnp.arange(half, dtype=jnp.float32) / half)
    ang = positions.astype(jnp.float32)[:, :, None, None] * inv_freq
    cos, sin = jnp.cos(ang), jnp.sin(ang)
    x1 = x[..., :half].astype(jnp.float32)
    x2 = x[..., half:].astype(jnp.float32)
    out = jnp.concatenate([x1 * cos - x2 * sin, x2 * cos + x1 * sin], axis=-1)
    return out.astype(x.dtype)


def t5_bucket(dist):
    n = jnp.maximum(dist, 0)
    max_exact = N_BUCKETS // 2
    nf = jnp.maximum(n, 1).astype(jnp.float32)
    large = max_exact + (jnp.log(nf / max_exact) / math.log(MAX_DISTANCE / max_exact)
                         * (N_BUCKETS - max_exact)).astype(jnp.int32)
    large = jnp.minimum(large, N_BUCKETS - 1)
    return jnp.where(n < max_exact, n, large)


def mla_group(q_lat, kv_lat, k_pe, positions, q_a_gain, w_q_b, kv_a_gain, w_kv_b, q_gain, k_gain):
    B, S, _ = q_lat.shape
    q = (rms_norm(q_lat, q_a_gain) @ w_q_b).reshape(B, S, MLA_HEADS, MLA_QK)
    kv = (rms_norm(kv_lat, kv_a_gain) @ w_kv_b).reshape(B, S, MLA_HEADS, MLA_NOPE + MLA_V)
    k_nope, v = kv[..., :MLA_NOPE], kv[..., MLA_NOPE:]
    k = jnp.concatenate([k_nope, jnp.broadcast_to(k_pe[:, :, None, :], (B, S, MLA_HEADS, MLA_ROPE))], axis=-1)
    q = rms_norm(q, q_gain)
    k = rms_norm(k, k_gain)
    q = jnp.concatenate([q[..., :MLA_NOPE], rope(q[..., MLA_NOPE:], positions)], axis=-1)
    k = jnp.concatenate([k[..., :MLA_NOPE], rope(k[..., MLA_NOPE:], positions)], axis=-1)
    scale = MLA_QK ** -0.5
    nb = S // BLOCK
    q_blocks = q.reshape(B, nb, BLOCK, MLA_HEADS, MLA_QK).transpose(1, 0, 2, 3, 4)
    k_idx = jnp.arange(S)

    def one_block(args):
        qb, bi = args
        s = jnp.einsum('bqhd,bkhd->bhqk', qb, k, preferred_element_type=jnp.float32) * scale
        q_idx = bi * BLOCK + jnp.arange(BLOCK)
        causal = k_idx[None, :] <= q_idx[:, None]
        s = jnp.where(causal[None, None], s, NEG_INF)
        p = jax.nn.softmax(s, axis=-1).astype(v.dtype)
        return jnp.einsum('bhqk,bkhd->bqhd', p, v)

    o = lax.map(one_block, (q_blocks, jnp.arange(nb)))
    return o.transpose(1, 0, 2, 3, 4).reshape(B, S, MLA_OUT)


def band(t, nb):
    B, S = t.shape[:2]
    rest = t.shape[2:]
    tp = jnp.pad(t, [(0, 0), (BLOCK, 0)] + [(0, 0)] * len(rest))
    prev = tp[:, :S].reshape(B, nb, BLOCK, *rest)
    cur = t.reshape(B, nb, BLOCK, *rest)
    return jnp.concatenate([prev, cur], axis=2)


def swa_group(q, k, v, positions, q_gain, k_gain, sinks, rel_table):
    B, S, _ = q.shape
    nb = S // BLOCK
    q = rms_norm(q.reshape(B, S, SWA_KV_HEADS, SWA_GROUP, SWA_HD), q_gain)
    k = rms_norm(k.reshape(B, S, SWA_KV_HEADS, SWA_HD), k_gain)
    v = v.reshape(B, S, SWA_KV_HEADS, SWA_HD)
    qb = q.reshape(B, nb, BLOCK, SWA_KV_HEADS, SWA_GROUP, SWA_HD)
    kb, vb, pb = band(k, nb), band(v, nb), band(positions, nb)
    s = jnp.einsum('bnqkgd,bnjkd->bnkgqj', qb, kb, preferred_element_type=jnp.float32) * (SWA_HD ** -0.5)
    qp = positions.reshape(B, nb, BLOCK)
    bucket = t5_bucket(qp[..., :, None] - pb[..., None, :])
    bias = rel_table[bucket].astype(jnp.float32)
    bias = bias.reshape(B, nb, BLOCK, 2 * BLOCK, SWA_KV_HEADS, SWA_GROUP).transpose(0, 1, 4, 5, 2, 3)
    s = s + bias
    q_idx = jnp.arange(S).reshape(nb, BLOCK)
    k_idx = (jnp.arange(nb) * BLOCK - BLOCK)[:, None] + jnp.arange(2 * BLOCK)[None, :]
    off = q_idx[:, :, None] - k_idx[:, None, :]
    mask = (off >= 0) & (off < WINDOW) & (k_idx[:, None, :] >= 0)
    s = jnp.where(mask[None, :, None, None], s, NEG_INF)
    sink = jnp.broadcast_to(sinks.astype(jnp.float32).reshape(SWA_KV_HEADS, SWA_GROUP)[None, None, :, :, None, None],
                            s.shape[:-1] + (1,))
    p = jax.nn.softmax(jnp.concatenate([s, sink], axis=-1), axis=-1)[..., :-1].astype(vb.dtype)
    o = jnp.einsum('bnkgqj,bnjkd->bnqkgd', p, vb)
    return o.reshape(B, S, SWA_OUT)


def peer(h, w_q, sub_keys, expert_u, expert_v):
    B, S, D = h.shape
    T = B * S
    xt = h.reshape(T, D)
    q = (xt @ w_q).reshape(T, PEER_HEADS, 2, PEER_DQ // 2)
    s = jnp.einsum('thpd,hpnd->thpn', q, sub_keys, preferred_element_type=jnp.float32)
    s_half, i_half = lax.top_k(s, PEER_TOPK)
    cand = (s_half[:, :, 0, :, None] + s_half[:, :, 1, None, :]).reshape(T, PEER_HEADS, PEER_TOPK * PEER_TOPK)
    cand_idx = (i_half[:, :, 0, :, None] * PEER_NKEYS + i_half[:, :, 1, None, :]).reshape(T, PEER_HEADS, PEER_TOPK * PEER_TOPK)
    top_s, pos = lax.top_k(cand, PEER_TOPK)
    idx = jnp.take_along_axis(cand_idx, pos, axis=-1)
    g = jax.nn.softmax(top_s, axis=-1)
    n_chunks = T // PEER_CHUNK
    HK = PEER_HEADS * PEER_TOPK

    def one_chunk(args):
        xc, ic, gc = args
        u = expert_u[ic]
        a = jnp.einsum('cd,ced->ce', xc, u, preferred_element_type=jnp.float32)
        w = (gc * jax.nn.gelu(a)).astype(xc.dtype)
        return jnp.einsum('ce,ced->cd', w, expert_v[ic])

    y = lax.map(one_chunk, (xt.reshape(n_chunks, PEER_CHUNK, D),
                            idx.reshape(n_chunks, PEER_CHUNK, HK),
                            g.reshape(n_chunks, PEER_CHUNK, HK)))
    return y.reshape(B, S, D)


def setup_inputs(seed: int = 0) -> dict:
    key = jax.random.key(seed)
    ks = jax.random.split(key, 24)
    f32 = jnp.float32
    L = DEPTH

    def nrm(k, shape, scale):
        return jax.random.normal(k, shape, f32) * scale

    def gain(k, shape):
        return 1.0 + 0.05 * jax.random.normal(k, shape, f32)

    x = jax.random.normal(ks[0], (BATCH, SEQ, D_MODEL), f32)
    offset = jax.random.randint(ks[1], (BATCH, 1), 0, 1024, dtype=jnp.int32)
    positions = (offset + jnp.arange(SEQ, dtype=jnp.int32)[None, :]).astype(jnp.int32)
    return {
        "x": x,
        "positions": positions,
        "norm1_gain": gain(ks[2], (L, D_MODEL)),
        "w_in": nrm(ks[3], (L, D_MODEL, IN_COLS), D_MODEL ** -0.5),
        "q_a_gain": gain(ks[4], (L, MLA_Q_RANK)),
        "w_q_b": nrm(ks[5], (L, MLA_Q_RANK, MLA_HEADS * MLA_QK), MLA_Q_RANK ** -0.5),
        "kv_a_gain": gain(ks[6], (L, MLA_KV_RANK)),
        "w_kv_b": nrm(ks[7], (L, MLA_KV_RANK, MLA_HEADS * (MLA_NOPE + MLA_V)), MLA_KV_RANK ** -0.5),
        "mla_q_gain": gain(ks[8], (L, MLA_QK)),
        "mla_k_gain": gain(ks[9], (L, MLA_QK)),
        "swa_q_gain": gain(ks[10], (L, SWA_HD)),
        "swa_k_gain": gain(ks[11], (L, SWA_HD)),
        "swa_sinks": nrm(ks[12], (L, SWA_HEADS), 1.0),
        "rel_bias_table": nrm(ks[13], (N_BUCKETS, SWA_HEADS), 0.5),
        "group_out_gain": gain(ks[14], (L, MIX_WIDTH)),
        "w_out": nrm(ks[15], (L, MIX_WIDTH, D_MODEL), MIX_WIDTH ** -0.5),
        "norm2_gain": gain(ks[16], (L, D_MODEL)),
        "peer_w_q": nrm(ks[17], (L, D_MODEL, PEER_HEADS * PEER_DQ), D_MODEL ** -0.5),
        "peer_sub_keys": nrm(ks[18], (L, PEER_HEADS, 2, PEER_NKEYS, PEER_DQ // 2), (PEER_DQ // 2) ** -0.5),
        "peer_u": nrm(ks[19], (L, PEER_N, D_MODEL), D_MODEL ** -0.5),
        "peer_v": nrm(ks[20], (L, PEER_N, D_MODEL), 0.3),
    }


def reference(x, positions, norm1_gain, w_in, q_a_gain, w_q_b, kv_a_gain, w_kv_b, mla_q_gain, mla_k_gain,
              swa_q_gain, swa_k_gain, swa_sinks, rel_bias_table, group_out_gain, w_out, norm2_gain,
              peer_w_q, peer_sub_keys, peer_u, peer_v):
    offsets = np.cumsum(IN_SPLITS)[:-1].tolist()
    h = x
    for l in range(DEPTH):
        n1 = rms_norm(h, norm1_gain[l])
        proj = n1 @ w_in[l]
        q_lat, kv_lat, k_pe, q_swa, k_swa, v_swa = jnp.split(proj, offsets, axis=-1)
        o_mla = mla_group(q_lat, kv_lat, k_pe, positions, q_a_gain[l], w_q_b[l], kv_a_gain[l], w_kv_b[l],
                          mla_q_gain[l], mla_k_gain[l])
        o_swa = swa_group(q_swa, k_swa, v_swa, positions, swa_q_gain[l], swa_k_gain[l], swa_sinks[l],
                          rel_bias_table)
        g_out = group_out_gain[l]
        mixed = jnp.concatenate([rms_norm(o_mla, g_out[:MLA_OUT]), rms_norm(o_swa, g_out[MLA_OUT:])], axis=-1)
        h = h + mixed @ w_out[l]
        h = h + peer(rms_norm(h, norm2_gain[l]), peer_w_q[l], peer_sub_keys[l], peer_u[l], peer_v[l])
    return h
```

```python
import functools
import math

import jax
import jax.numpy as jnp
import numpy as np
from jax import lax
from jax.experimental import pallas as pl
from jax.experimental.pallas import tpu as pltpu

EPS = 1e-6
NEG_INF = -1e30
LANES = 128
VMEM_LIMIT = 56 << 20

MLA_HEADS = 8
MLA_NOPE = 128
MLA_ROPE = 64
MLA_V = 128
MLA_QK = MLA_NOPE + MLA_ROPE
MLA_QK_PAD = 256
MLA_Q_RANK = 512
MLA_KV_RANK = 256
ROPE_THETA = 10000.0

SWA_HEADS = 16
SWA_KV_HEADS = 2
SWA_HD = 64
SWA_GROUP = SWA_HEADS // SWA_KV_HEADS
WINDOW = 128
BLOCK = 128
N_BUCKETS = 32
MAX_DISTANCE = 128

PEER_HEADS = 8
PEER_NKEYS = 128
PEER_TOPK = 16
PEER_HALF = 128

BF16 = jnp.bfloat16
F32 = jnp.float32


def _resident(shape):
    nd = len(shape)
    return pl.BlockSpec(shape, lambda *_: (0,) * nd, pipeline_mode=pl.Buffered(1))


def _rms_scale(x, width):
    return lax.rsqrt(jnp.sum(x * x, axis=-1, keepdims=True) * (1.0 / width) + EPS)


def _dot(a, b):
    return jnp.dot(a, b, preferred_element_type=F32)


def _dot_nt(a, b):
    return lax.dot_general(a, b, (((1,), (1,)), ((), ())), preferred_element_type=F32)


_C_QLAT = 0
_C_KVLAT = _C_QLAT + MLA_Q_RANK
_C_KPE = _C_KVLAT + MLA_KV_RANK
_C_QSWA = _C_KPE + LANES
_C_KSWA = _C_QSWA + SWA_HEADS * SWA_HD
_C_VSWA = _C_KSWA + 2 * LANES
_C_END = _C_VSWA + 2 * LANES


def _rope(x, cos_t, sin_t):
    partner = pltpu.roll(x, 32, axis=1) + pltpu.roll(x, 96, axis=1)
    return x * cos_t + partner * sin_t


def _in_proj_kernel(x_ref, pos_ref, g1_ref, win_ref, qag_ref, wqb_ref, kvag_ref, wkvb_ref,
                    qg_ref, kg_ref, sqg_ref, skg_ref, rc_ref,
                    qm_ref, km_ref, vm_ref, qs_ref, ks_ref, vs_ref):
    x = x_ref[0]
    n1 = x * _rms_scale(x, x.shape[-1]) * g1_ref[...]
    proj = _dot(n1.astype(BF16), win_ref[...])

    pos = pos_ref[0].astype(F32)
    ang = pos * rc_ref[0:1, :]
    cos_t = jnp.cos(ang) * rc_ref[1:2, :]
    sin_t = jnp.sin(ang) * rc_ref[2:3, :]

    q_lat = proj[:, _C_QLAT:_C_QLAT + MLA_Q_RANK]
    ql = q_lat * _rms_scale(q_lat, MLA_Q_RANK) * qag_ref[...]
    q = _dot(ql.astype(BF16), wqb_ref[...])
    q_scale = MLA_QK ** -0.5
    for h in range(MLA_HEADS):
        qh = q[:, h * MLA_QK_PAD:(h + 1) * MLA_QK_PAD]
        qn = qh * _rms_scale(qh, MLA_QK) * qg_ref[...]
        qr = _rope(qn[:, LANES:], cos_t, sin_t)
        qm_ref[0, h, :, 0:LANES] = (qn[:, :LANES] * q_scale).astype(BF16)
        qm_ref[0, h, :, LANES:] = (qr * q_scale).astype(BF16)

    kv_lat = proj[:, _C_KVLAT:_C_KVLAT + MLA_KV_RANK]
    kvl = kv_lat * _rms_scale(kv_lat, MLA_KV_RANK) * kvag_ref[...]
    kv = _dot(kvl.astype(BF16), wkvb_ref[...])
    kpe = proj[:, _C_KPE:_C_KPE + LANES]
    kpe_ss = jnp.sum(kpe * kpe, axis=-1, keepdims=True)
    kr = _rope(kpe * kg_ref[:, LANES:], cos_t, sin_t)
    for h in range(MLA_HEADS):
        kn = kv[:, h * 256:h * 256 + MLA_NOPE]
        ss = jnp.sum(kn * kn, axis=-1, keepdims=True) + kpe_ss
        r = lax.rsqrt(ss * (1.0 / MLA_QK) + EPS)
        km_ref[0, h, :, 0:LANES] = (kn * r * kg_ref[:, :LANES]).astype(BF16)
        km_ref[0, h, :, LANES:] = (kr * r).astype(BF16)
        vm_ref[0, h] = kv[:, h * 256 + MLA_NOPE:(h + 1) * 256].astype(BF16)

    lane = lax.broadcasted_iota(jnp.int32, (1, LANES), 1)
    lo = lane < SWA_HD
    s_scale = SWA_HD ** -0.5
    for p in range(SWA_HEADS // 2):
        v = proj[:, _C_QSWA + p * LANES:_C_QSWA + (p + 1) * LANES]
        sq = v * v
        ss_lo = jnp.sum(jnp.where(lo, sq, 0.0), axis=-1, keepdims=True)
        ss_hi = jnp.sum(jnp.where(lo, 0.0, sq), axis=-1, keepdims=True)
        r = jnp.where(lo, lax.rsqrt(ss_lo * (1.0 / SWA_HD) + EPS), lax.rsqrt(ss_hi * (1.0 / SWA_HD) + EPS))
        qs_ref[0, :, p * LANES:(p + 1) * LANES] = (v * r * sqg_ref[...] * s_scale).astype(BF16)
    for g in range(SWA_KV_HEADS):
        v = proj[:, _C_KSWA + g * LANES:_C_KSWA + (g + 1) * LANES]
        r = lax.rsqrt(jnp.sum(v * v, axis=-1, keepdims=True) * (0.5 / SWA_HD) + EPS)
        ks_ref[0, :, g * LANES:(g + 1) * LANES] = (v * r * skg_ref[...]).astype(BF16)
    vs_ref[0] = proj[:, _C_VSWA:_C_END].astype(BF16)


def _in_proj(x, pos_col, g1, win, qag, wqb, kvag, wkvb, qg, kg, sqg, skg, rc, *, tm):
    B, S, D = x.shape
    grid = (B, S // tm)
    tok = lambda w: pl.BlockSpec((1, tm, w), lambda b, s: (b, s, 0))
    heads = lambda w: pl.BlockSpec((1, MLA_HEADS, tm, w), lambda b, s: (b, 0, s, 0))
    return pl.pallas_call(
        _in_proj_kernel,
        grid=grid,
        in_specs=[tok(D), tok(1)] + [_resident(a.shape) for a in (g1, win, qag, wqb, kvag, wkvb, qg, kg, sqg, skg, rc)],
        out_specs=[heads(MLA_QK_PAD), heads(MLA_QK_PAD), heads(MLA_V), tok(SWA_HEADS * SWA_HD), tok(2 * LANES), tok(2 * LANES)],
        out_shape=[
            jax.ShapeDtypeStruct((B, MLA_HEADS, S, MLA_QK_PAD), BF16),
            jax.ShapeDtypeStruct((B, MLA_HEADS, S, MLA_QK_PAD), BF16),
            jax.ShapeDtypeStruct((B, MLA_HEADS, S, MLA_V), BF16),
            jax.ShapeDtypeStruct((B, S, SWA_HEADS * SWA_HD), BF16),
            jax.ShapeDtypeStruct((B, S, 2 * LANES), BF16),
            jax.ShapeDtypeStruct((B, S, 2 * LANES), BF16),
        ],
        compiler_params=pltpu.CompilerParams(dimension_semantics=("parallel", "parallel"), vmem_limit_bytes=VMEM_LIMIT),
        name="in_proj",
    )(x, pos_col, g1, win, qag, wqb, kvag, wkvb, qg, kg, sqg, skg, rc)


def _mla_attn_kernel(q_ref, k_ref, v_ref, o_ref, *, tq):
    qi = pl.program_id(2)
    q = q_ref[0, 0]

    def chunk(j, carry, masked):
        m, l, acc = carry
        start = pl.multiple_of(j * tq, tq)
        k = k_ref[0, 0, pl.ds(start, tq), :]
        v = v_ref[0, 0, pl.ds(start, tq), :]
        s = _dot_nt(q, k)
        if masked:
            row = lax.broadcasted_iota(jnp.int32, (tq, tq), 0)
            col = lax.broadcasted_iota(jnp.int32, (tq, tq), 1)
            s = jnp.where(col <= row, s, NEG_INF)
        m_new = jnp.maximum(m, jnp.max(s, axis=-1, keepdims=True))
        alpha = jnp.exp(m - m_new)
        p = jnp.exp(s - m_new)
        l = alpha * l + jnp.sum(p, axis=-1, keepdims=True)
        acc = alpha * acc + _dot(p.astype(BF16), v)
        return m_new, l, acc

    init = (jnp.full((tq, 1), NEG_INF, F32), jnp.zeros((tq, 1), F32), jnp.zeros((tq, MLA_V), F32))
    carry = lax.fori_loop(0, qi, lambda j, c: chunk(j, c, False), init)
    m, l, acc = chunk(qi, carry, True)
    o_ref[0] = acc / l


def _mla_attn(qm, km, vm, *, tq):
    B, H, S, _ = qm.shape
    return pl.pallas_call(
        functools.partial(_mla_attn_kernel, tq=tq),
        grid=(B, H, S // tq),
        in_specs=[
            pl.BlockSpec((1, 1, tq, MLA_QK_PAD), lambda b, h, i: (b, h, i, 0)),
            pl.BlockSpec((1, 1, S, MLA_QK_PAD), lambda b, h, i: (b, h, 0, 0)),
            pl.BlockSpec((1, 1, S, MLA_V), lambda b, h, i: (b, h, 0, 0)),
        ],
        out_specs=pl.BlockSpec((1, tq, MLA_V), lambda b, h, i: (b, i, h)),
        out_shape=jax.ShapeDtypeStruct((B, S, H * MLA_V), F32),
        compiler_params=pltpu.CompilerParams(dimension_semantics=("parallel", "parallel", "arbitrary"),
                                             vmem_limit_bytes=VMEM_LIMIT),
        name="mla_attn",
    )(qm, km, vm)


def _t5_bucket(dist):
    n = jnp.maximum(dist, 0)
    max_exact = N_BUCKETS // 2
    nf = jnp.maximum(n, 1).astype(F32)
    large = max_exact + (jnp.log(nf / max_exact) / math.log(MAX_DISTANCE / max_exact)
                         * (N_BUCKETS - max_exact)).astype(jnp.int32)
    large = jnp.minimum(large, N_BUCKETS - 1)
    return jnp.where(n < max_exact, n, large)


def _swa_attn_kernel(sink_ref, q_ref, kp_ref, kc_ref, vp_ref, vc_ref, posq_ref, pkp_ref, pkc_ref, tab_ref, o_ref):
    n = pl.program_id(1)
    kb = jnp.concatenate([kp_ref[0], kc_ref[0]], axis=0)
    vb = jnp.concatenate([vp_ref[0], vc_ref[0]], axis=0)
    kpos = jnp.concatenate([pkp_ref[0], pkc_ref[0]], axis=1)
    bucket = _t5_bucket(posq_ref[0] - kpos)

    row = lax.broadcasted_iota(jnp.int32, (BLOCK, 2 * BLOCK), 0)
    col = lax.broadcasted_iota(jnp.int32, (BLOCK, 2 * BLOCK), 1)
    off = row + BLOCK - col
    valid = (off >= 0) & (off < WINDOW) & ((col >= BLOCK) | (n > 0))

    lane = lax.broadcasted_iota(jnp.int32, (1, LANES), 1)
    lo = lane < SWA_HD
    zero = jnp.zeros((), BF16)

    outs = []
    for pair in range(SWA_HEADS // 2):
        g = (2 * pair) // SWA_GROUP
        qp = q_ref[0, :, pair * LANES:(pair + 1) * LANES]
        kg = kb[:, g * LANES:(g + 1) * LANES]
        vg = vb[:, g * LANES:(g + 1) * LANES]
        o_pair = None
        for half in range(2):
            h = 2 * pair + half
            sel = lo if half == 0 else jnp.logical_not(lo)
            s = _dot_nt(qp, jnp.where(sel, kg, zero))
            tab_row = jnp.broadcast_to(tab_ref[h:h + 1, :], (BLOCK, LANES))
            bias = jnp.concatenate(
                [jnp.take_along_axis(tab_row, bucket[:, c * LANES:(c + 1) * LANES], axis=1) for c in range(2)], axis=1)
            s = jnp.where(valid, s + bias, NEG_INF)
            sink = sink_ref[h]
            m = jnp.maximum(jnp.max(s, axis=-1, keepdims=True), sink)
            p = jnp.exp(s - m)
            denom = jnp.sum(p, axis=-1, keepdims=True) + jnp.exp(sink - m)
            p = (p / denom).astype(BF16)
            o = _dot(p, jnp.where(sel, vg, zero))
            o_pair = o if o_pair is None else o_pair + o
        outs.append(o_pair)
    o_ref[0] = jnp.concatenate(outs, axis=1)


def _swa_attn(sinks, qs, ks, vs, pos_col, pos_row, tab):
    B, S, _ = qs.shape
    nb = S // BLOCK
    cur = lambda w: pl.BlockSpec((1, BLOCK, w), lambda b, n: (b, n, 0))
    prev = lambda w: pl.BlockSpec((1, BLOCK, w), lambda b, n: (b, jnp.maximum(n - 1, 0), 0))
    return pl.pallas_call(
        _swa_attn_kernel,
        grid=(B, nb),
        in_specs=[
            pl.BlockSpec(memory_space=pltpu.SMEM),
            cur(SWA_HEADS * SWA_HD), prev(2 * LANES), cur(2 * LANES), prev(2 * LANES), cur(2 * LANES),
            cur(1),
            pl.BlockSpec((1, 1, BLOCK), lambda b, n: (b, 0, jnp.maximum(n - 1, 0))),
            pl.BlockSpec((1, 1, BLOCK), lambda b, n: (b, 0, n)),
            _resident(tab.shape),
        ],
        out_specs=cur(SWA_HEADS * SWA_HD),
        out_shape=jax.ShapeDtypeStruct((B, S, SWA_HEADS * SWA_HD), F32),
        compiler_params=pltpu.CompilerParams(dimension_semantics=("parallel", "arbitrary"), vmem_limit_bytes=VMEM_LIMIT),
        name="swa_attn",
    )(sinks, qs, ks, ks, vs, vs, pos_col, pos_row, pos_row, tab)


def _out_proj_kernel(om_ref, os_ref, x_ref, gout_ref, wout_ref, g2_ref, h_ref, n2_ref):
    om = om_ref[...]
    osw = os_ref[...]
    half = om.shape[-1]
    a = om * _rms_scale(om, half) * gout_ref[:, :half]
    b = osw * _rms_scale(osw, half) * gout_ref[:, half:]
    mixed = jnp.concatenate([a.astype(BF16), b.astype(BF16)], axis=1)
    h = x_ref[...] + _dot(mixed, wout_ref[...])
    h_ref[...] = h
    n2_ref[...] = (h * _rms_scale(h, h.shape[-1]) * g2_ref[...]).astype(BF16)


def _out_proj(om, osw, x2, gout, wout, g2, *, tm):
    T, D = x2.shape
    half = om.shape[-1]
    tok = lambda w: pl.BlockSpec((tm, w), lambda t: (t, 0))
    return pl.pallas_call(
        _out_proj_kernel,
        grid=(T // tm,),
        in_specs=[tok(half), tok(half), tok(D), _resident(gout.shape), _resident(wout.shape), _resident(g2.shape)],
        out_specs=[tok(D), tok(D)],
        out_shape=[jax.ShapeDtypeStruct((T, D), F32), jax.ShapeDtypeStruct((T, D), BF16)],
        compiler_params=pltpu.CompilerParams(dimension_semantics=("parallel",), vmem_limit_bytes=VMEM_LIMIT),
        name="out_proj",
    )(om, osw, x2, gout, wout, g2)


_CAND = [(a, b) for a in range(PEER_TOPK) for b in range(PEER_TOPK) if (a + 1) * (b + 1) <= PEER_TOPK]


def _peer_route_kernel(n2_ref, wqt_ref, sk_ref, cnt_ref, a_ref, r2_ref, b_ref,
                       s_scr, rank_scr, vals_scr, n_scr, m_scr, z_scr, *, tr):
    n_lt = tr // LANES
    n_hp = 2 * PEER_HEADS
    qt = lax.dot_general(wqt_ref[...], n2_ref[...], (((1,), (1,)), ((), ())),
                         preferred_element_type=F32).astype(BF16)
    for hp in range(n_hp):
        s_scr[hp] = _dot(sk_ref[hp], qt[hp * PEER_HALF:(hp + 1) * PEER_HALF, :])

    key = lax.broadcasted_iota(jnp.int32, (PEER_NKEYS, LANES), 0).astype(F32)
    head = lax.broadcasted_iota(jnp.int32, (PEER_HEADS, LANES), 0)
    vals_scr[...] = jnp.zeros_like(vals_scr)

    def top16(hp, _):
        h = hp // 2
        p = hp % 2
        for lt in range(n_lt):
            lanes = slice(lt * LANES, (lt + 1) * LANES)
            v = s_scr[hp, :, lanes]
            rank = jnp.full((PEER_NKEYS, LANES), float(PEER_TOPK), F32)
            for r in range(PEER_TOPK):
                m = jnp.max(v, axis=0, keepdims=True)
                first = jnp.min(jnp.where(v == m, key, float(PEER_NKEYS)), axis=0, keepdims=True)
                hit = key == first
                v = jnp.where(hit, -jnp.inf, v)
                rank = jnp.where(hit, float(r), rank)
                vals_scr[p, r, :, lanes] = jnp.where(head == h, m, vals_scr[p, r, :, lanes])
            rank_scr[hp, :, lanes] = rank
        return 0

    lax.fori_loop(0, n_hp, top16, 0)

    for lt in range(n_lt):
        lanes = slice(lt * LANES, (lt + 1) * LANES)
        v1 = [vals_scr[0, a, :, lanes] for a in range(PEER_TOPK)]
        v2 = [vals_scr[1, b, :, lanes] for b in range(PEER_TOPK)]
        c = [v1[a] + v2[b] for (a, b) in _CAND]
        flat = [float(a * PEER_TOPK + b) for (a, b) in _CAND]
        for _ in range(PEER_TOPK):
            m = functools.reduce(jnp.maximum, c)
            first = functools.reduce(jnp.minimum, [jnp.where(ci == m, fi, 1e9) for ci, fi in zip(c, flat)])
            c = [jnp.where(first == fi, -jnp.inf, ci) for ci, fi in zip(c, flat)]
        e1 = [jnp.exp(v1[a] - v1[0]) for a in range(PEER_TOPK)]
        e2 = [jnp.exp(v2[b] - v2[0]) for b in range(PEER_TOPK)]
        z = jnp.zeros_like(v1[0])
        n_a = [jnp.zeros_like(v1[0]) for _ in range(PEER_TOPK)]
        for ci, (a, b) in zip(c, _CAND):
            taken = ci == -jnp.inf
            z = z + jnp.where(taken, e1[a] * e2[b], 0.0)
            n_a[a] = n_a[a] + jnp.where(taken, 1.0, 0.0)
        for a in range(PEER_TOPK):
            n_scr[a, :, lanes] = n_a[a]
        m_scr[0, :, lanes] = v1[0]
        m_scr[1, :, lanes] = v2[0]
        z_scr[:, lanes] = 1.0 / z

    def spread(h, _):
        rank1 = rank_scr[2 * h]
        cnt = jnp.zeros((PEER_NKEYS, tr), F32)
        for a in range(PEER_TOPK):
            cnt = jnp.where(rank1 == float(a), n_scr[a, pl.ds(h, 1), :], cnt)
        cnt_ref[h] = cnt
        a_ref[h] = jnp.exp(s_scr[2 * h] - m_scr[0, pl.ds(h, 1), :]) * z_scr[pl.ds(h, 1), :]
        r2_ref[h] = rank_scr[2 * h + 1]
        b_ref[h] = jnp.exp(s_scr[2 * h + 1] - m_scr[1, pl.ds(h, 1), :])
        return 0

    lax.fori_loop(0, PEER_HEADS, spread, 0)


def _peer_route(n2, wqt, sk, *, tr):
    T, D = n2.shape
    route = pl.BlockSpec((PEER_HEADS, PEER_NKEYS, tr), lambda t: (0, 0, t))
    shape = jax.ShapeDtypeStruct((PEER_HEADS, PEER_NKEYS, T), F32)
    return pl.pallas_call(
        functools.partial(_peer_route_kernel, tr=tr),
        grid=(T // tr,),
        in_specs=[pl.BlockSpec((tr, D), lambda t: (t, 0)), _resident(wqt.shape), _resident(sk.shape)],
        out_specs=[route] * 4,
        out_shape=[shape] * 4,
        scratch_shapes=[
            pltpu.VMEM((2 * PEER_HEADS, PEER_NKEYS, tr), F32),
            pltpu.VMEM((2 * PEER_HEADS, PEER_NKEYS, tr), F32),
            pltpu.VMEM((2, PEER_TOPK, PEER_HEADS, tr), F32),
            pltpu.VMEM((PEER_TOPK, PEER_HEADS, tr), F32),
            pltpu.VMEM((2, PEER_HEADS, tr), F32),
            pltpu.VMEM((PEER_HEADS, tr), F32),
        ],
        compiler_params=pltpu.CompilerParams(dimension_semantics=("parallel",), vmem_limit_bytes=VMEM_LIMIT),
        name="peer_route",
    )(n2, wqt, sk)


def _peer_dense_kernel(n2_ref, u_ref, vt_ref, cnt_ref, a_ref, r2_ref, b_ref, h_ref, o_ref, acc_ref, *, te):
    e = pl.program_id(1)

    @pl.when(e == 0)
    def _():
        acc_ref[...] = jnp.zeros_like(acc_ref)

    act = jax.nn.gelu(_dot_nt(u_ref[...], n2_ref[...]))
    gates = []
    for ii in range(te // PEER_NKEYS):
        i = e * (te // PEER_NKEYS) + ii
        gate = None
        for h in range(PEER_HEADS):
            take = r2_ref[h] < cnt_ref[h, pl.ds(i, 1), :]
            term = jnp.where(take, b_ref[h], 0.0) * a_ref[h, pl.ds(i, 1), :]
            gate = term if gate is None else gate + term
        gates.append(gate)
    w = (act * jnp.concatenate(gates, axis=0)).astype(BF16)
    acc_ref[...] += _dot(vt_ref[...], w)

    @pl.when(e == pl.num_programs(1) - 1)
    def _():
        o_ref[...] = h_ref[...] + acc_ref[...].T


def _peer_dense(n2, u, vt, cnt, a, r2, b, h, *, tm, te):
    T, D = n2.shape
    N = u.shape[0]
    route = pl.BlockSpec((PEER_HEADS, PEER_NKEYS, tm), lambda t, e: (0, 0, t))
    return pl.pallas_call(
        functools.partial(_peer_dense_kernel, te=te),
        grid=(T // tm, N // te),
        in_specs=[
            pl.BlockSpec((tm, D), lambda t, e: (t, 0)),
            pl.BlockSpec((te, D), lambda t, e: (e, 0)),
            pl.BlockSpec((D, te), lambda t, e: (0, e)),
            route, route, route, route,
            pl.BlockSpec((tm, D), lambda t, e: (t, 0)),
        ],
        out_specs=pl.BlockSpec((tm, D), lambda t, e: (t, 0)),
        out_shape=jax.ShapeDtypeStruct((T, D), F32),
        scratch_shapes=[pltpu.VMEM((D, tm), F32)],
        compiler_params=pltpu.CompilerParams(dimension_semantics=("parallel", "arbitrary"), vmem_limit_bytes=VMEM_LIMIT),
        name="peer_dense",
    )(n2, u, vt, cnt, a, r2, b, h)


def _row(v, width=None):
    v = v.astype(F32).reshape(1, -1)
    if width is not None and v.shape[1] < width:
        v = jnp.pad(v, ((0, 0), (0, width - v.shape[1])))
    return v


def _rope_consts():
    half = MLA_ROPE // 2
    inv_freq = ROPE_THETA ** (-jnp.arange(half, dtype=F32) / half)
    z = jnp.zeros((LANES - MLA_ROPE,), F32)
    rows = [
        jnp.concatenate([inv_freq, inv_freq, z]),
        jnp.concatenate([jnp.ones((MLA_ROPE,), F32), z]),
        jnp.concatenate([-jnp.ones((half,), F32), jnp.ones((half,), F32), z]),
    ]
    return jnp.pad(jnp.stack(rows), ((0, 5), (0, 0)))


def _layer(x, positions, norm1_gain, w_in, q_a_gain, w_q_b, kv_a_gain, w_kv_b, mla_q_gain, mla_k_gain,
           swa_q_gain, swa_k_gain, swa_sinks, rel_bias_table, group_out_gain, w_out, norm2_gain,
           peer_w_q, peer_sub_keys, peer_u, peer_v, *, tm_in, tq, tm_out, tr, tm_peer, te):
    B, S, D = x.shape
    T = B * S

    zpad = jnp.zeros((D, LANES - MLA_ROPE), w_in.dtype)
    o = np.cumsum((MLA_Q_RANK, MLA_KV_RANK, MLA_ROPE, SWA_HEADS * SWA_HD, SWA_KV_HEADS * SWA_HD)).tolist()
    k_swa, v_swa = w_in[:, o[3]:o[4]], w_in[:, o[4]:]
    dup = lambda w: jnp.concatenate([w[:, :SWA_HD], w[:, :SWA_HD], w[:, SWA_HD:], w[:, SWA_HD:]], axis=1)
    win = jnp.concatenate([w_in[:, :o[2]], zpad, w_in[:, o[2]:o[3]], dup(k_swa), dup(v_swa)], axis=1).astype(BF16)
    wqb = jnp.pad(w_q_b.reshape(MLA_Q_RANK, MLA_HEADS, MLA_QK), ((0, 0), (0, 0), (0, MLA_QK_PAD - MLA_QK)))
    wqb = wqb.reshape(MLA_Q_RANK, MLA_HEADS * MLA_QK_PAD).astype(BF16)

    qm, km, vm, qs, ks, vs = _in_proj(
        x, positions.reshape(B, S, 1), _row(norm1_gain), win, _row(q_a_gain), wqb, _row(kv_a_gain),
        w_kv_b.astype(BF16), _row(mla_q_gain, MLA_QK_PAD), _row(mla_k_gain, MLA_QK_PAD),
        _row(jnp.tile(swa_q_gain, 2)), _row(jnp.tile(swa_k_gain, 2)), _rope_consts(), tm=tm_in)

    o_mla = _mla_attn(qm, km, vm, tq=tq)
    tab = jnp.pad(rel_bias_table.astype(F32).T, ((0, 0), (0, LANES - N_BUCKETS)))
    o_swa = _swa_attn(swa_sinks.astype(F32), qs, ks, vs, positions.reshape(B, S, 1), positions.reshape(B, 1, S), tab)

    h, n2 = _out_proj(o_mla.reshape(T, -1), o_swa.reshape(T, -1), x.reshape(T, D), _row(group_out_gain),
                      w_out.astype(BF16), _row(norm2_gain), tm=tm_out)

    sk = peer_sub_keys.reshape(2 * PEER_HEADS, PEER_NKEYS, PEER_HALF).astype(BF16)
    cnt, a, r2, b = _peer_route(n2, peer_w_q.T.astype(BF16), sk, tr=tr)
    out = _peer_dense(n2, peer_u.astype(BF16), peer_v.T.astype(BF16), cnt, a, r2, b, h, tm=tm_peer, te=te)
    return out.reshape(B, S, D)


def kernel(x, positions, norm1_gain, w_in, q_a_gain, w_q_b, kv_a_gain, w_kv_b, mla_q_gain, mla_k_gain, swa_q_gain, swa_k_gain, swa_sinks, rel_bias_table, group_out_gain, w_out, norm2_gain, peer_w_q, peer_sub_keys, peer_u, peer_v):
    assert norm1_gain.shape[0] == 1, "single-layer trunk"
    return _layer(x, positions, norm1_gain[0], w_in[0], q_a_gain[0], w_q_b[0], kv_a_gain[0], w_kv_b[0],
                  mla_q_gain[0], mla_k_gain[0], swa_q_gain[0], swa_k_gain[0], swa_sinks[0], rel_bias_table,
                  group_out_gain[0], w_out[0], norm2_gain[0], peer_w_q[0], peer_sub_keys[0], peer_u[0], peer_v[0],
                  tm_in=256, tq=512, tm_out=512, tr=256, tm_peer=512, te=512)
```

```python
import functools
import math

import jax
import jax.numpy as jnp
import numpy as np
from jax import lax
from jax.experimental import pallas as pl
from jax.experimental.pallas import tpu as pltpu

EPS = 1e-6
NEG_INF = -1e30
LANES = 128
VMEM_LIMIT = 56 << 20

MLA_HEADS = 8
MLA_NOPE = 128
MLA_ROPE = 64
MLA_V = 128
MLA_QK = MLA_NOPE + MLA_ROPE
MLA_QK_PAD = 256
MLA_Q_RANK = 512
MLA_KV_RANK = 256
ROPE_THETA = 10000.0

SWA_HEADS = 16
SWA_KV_HEADS = 2
SWA_HD = 64
SWA_GROUP = SWA_HEADS // SWA_KV_HEADS
WINDOW = 128
BLOCK = 128
N_BUCKETS = 32
MAX_DISTANCE = 128

PEER_HEADS = 8
PEER_NKEYS = 128
PEER_TOPK = 16
PEER_HALF = 128

BF16 = jnp.bfloat16
F32 = jnp.float32


def _resident(shape):
    nd = len(shape)
    return pl.BlockSpec(shape, lambda *_: (0,) * nd, pipeline_mode=pl.Buffered(1))


def _rms_scale(x, width):
    return lax.rsqrt(jnp.sum(x * x, axis=-1, keepdims=True) * (1.0 / width) + EPS)


def _dot(a, b):
    return jnp.dot(a, b, preferred_element_type=F32)


def _dot_nt(a, b):
    return lax.dot_general(a, b, (((1,), (1,)), ((), ())), preferred_element_type=F32)


_C_QLAT = 0
_C_KVLAT = _C_QLAT + MLA_Q_RANK
_C_KPE = _C_KVLAT + MLA_KV_RANK
_C_QSWA = _C_KPE + LANES
_C_KSWA = _C_QSWA + SWA_HEADS * SWA_HD
_C_VSWA = _C_KSWA + 2 * LANES
_C_END = _C_VSWA + 2 * LANES


def _rope(x, cos_t, sin_t):
    partner = pltpu.roll(x, 32, axis=1) + pltpu.roll(x, 96, axis=1)
    return x * cos_t + partner * sin_t


def _in_proj_kernel(x_ref, pos_ref, g1_ref, win_ref, qag_ref, wqb_ref, kvag_ref, wkvb_ref,
                    qg_ref, kg_ref, sqg_ref, skg_ref, rc_ref,
                    qm_ref, km_ref, vm_ref, qs_ref, ks_ref, vs_ref):
    x = x_ref[0]
    n1 = x * _rms_scale(x, x.shape[-1]) * g1_ref[...]
    proj = _dot(n1.astype(BF16), win_ref[...])

    pos = pos_ref[0].astype(F32)
    ang = pos * rc_ref[0:1, :]
    cos_t = jnp.cos(ang) * rc_ref[1:2, :]
    sin_t = jnp.sin(ang) * rc_ref[2:3, :]

    q_lat = proj[:, _C_QLAT:_C_QLAT + MLA_Q_RANK]
    ql = q_lat * _rms_scale(q_lat, MLA_Q_RANK) * qag_ref[...]
    q = _dot(ql.astype(BF16), wqb_ref[...])
    q_scale = MLA_QK ** -0.5
    for h in range(MLA_HEADS):
        qh = q[:, h * MLA_QK_PAD:(h + 1) * MLA_QK_PAD]
        qn = qh * _rms_scale(qh, MLA_QK) * qg_ref[...]
        qr = _rope(qn[:, LANES:], cos_t, sin_t)
        qm_ref[0, h, :, 0:LANES] = (qn[:, :LANES] * q_scale).astype(BF16)
        qm_ref[0, h, :, LANES:] = (qr * q_scale).astype(BF16)

    kv_lat = proj[:, _C_KVLAT:_C_KVLAT + MLA_KV_RANK]
    kvl = kv_lat * _rms_scale(kv_lat, MLA_KV_RANK) * kvag_ref[...]
    kv = _dot(kvl.astype(BF16), wkvb_ref[...])
    kpe = proj[:, _C_KPE:_C_KPE + LANES]
    kpe_ss = jnp.sum(kpe * kpe, axis=-1, keepdims=True)
    kr = _rope(kpe * kg_ref[:, LANES:], cos_t, sin_t)
    for h in range(MLA_HEADS):
        kn = kv[:, h * 256:h * 256 + MLA_NOPE]
        ss = jnp.sum(kn * kn, axis=-1, keepdims=True) + kpe_ss
        r = lax.rsqrt(ss * (1.0 / MLA_QK) + EPS)
        km_ref[0, h, :, 0:LANES] = (kn * r * kg_ref[:, :LANES]).astype(BF16)
        km_ref[0, h, :, LANES:] = (kr * r).astype(BF16)
        vm_ref[0, h] = kv[:, h * 256 + MLA_NOPE:(h + 1) * 256].astype(BF16)

    lane = lax.broadcasted_iota(jnp.int32, (1, LANES), 1)
    lo = lane < SWA_HD
    s_scale = SWA_HD ** -0.5
    for p in range(SWA_HEADS // 2):
        v = proj[:, _C_QSWA + p * LANES:_C_QSWA + (p + 1) * LANES]
        sq = v * v
        ss_lo = jnp.sum(jnp.where(lo, sq, 0.0), axis=-1, keepdims=True)
        ss_hi = jnp.sum(jnp.where(lo, 0.0, sq), axis=-1, keepdims=True)
        r = jnp.where(lo, lax.rsqrt(ss_lo * (1.0 / SWA_HD) + EPS), lax.rsqrt(ss_hi * (1.0 / SWA_HD) + EPS))
        qs_ref[0, :, p * LANES:(p + 1) * LANES] = (v * r * sqg_ref[...] * s_scale).astype(BF16)
    for g in range(SWA_KV_HEADS):
        v = proj[:, _C_KSWA + g * LANES:_C_KSWA + (g + 1) * LANES]
        r = lax.rsqrt(jnp.sum(v * v, axis=-1, keepdims=True) * (0.5 / SWA_HD) + EPS)
        ks_ref[0, :, g * LANES:(g + 1) * LANES] = (v * r * skg_ref[...]).astype(BF16)
    vs_ref[0] = proj[:, _C_VSWA:_C_END].astype(BF16)


def _in_proj(x, pos_col, g1, win, qag, wqb, kvag, wkvb, qg, kg, sqg, skg, rc, *, tm):
    B, S, D = x.shape
    grid = (B, S // tm)
    tok = lambda w: pl.BlockSpec((1, tm, w), lambda b, s: (b, s, 0))
    heads = lambda w: pl.BlockSpec((1, MLA_HEADS, tm, w), lambda b, s: (b, 0, s, 0))
    return pl.pallas_call(
        _in_proj_kernel,
        grid=grid,
        in_specs=[tok(D), tok(1)] + [_resident(a.shape) for a in (g1, win, qag, wqb, kvag, wkvb, qg, kg, sqg, skg, rc)],
        out_specs=[heads(MLA_QK_PAD), heads(MLA_QK_PAD), heads(MLA_V), tok(SWA_HEADS * SWA_HD), tok(2 * LANES), tok(2 * LANES)],
        out_shape=[
            jax.ShapeDtypeStruct((B, MLA_HEADS, S, MLA_QK_PAD), BF16),
            jax.ShapeDtypeStruct((B, MLA_HEADS, S, MLA_QK_PAD), BF16),
            jax.ShapeDtypeStruct((B, MLA_HEADS, S, MLA_V), BF16),
            jax.ShapeDtypeStruct((B, S, SWA_HEADS * SWA_HD), BF16),
            jax.ShapeDtypeStruct((B, S, 2 * LANES), BF16),
            jax.ShapeDtypeStruct((B, S, 2 * LANES), BF16),
        ],
        compiler_params=pltpu.CompilerParams(dimension_semantics=("parallel", "parallel"), vmem_limit_bytes=VMEM_LIMIT),
        name="in_proj",
    )(x, pos_col, g1, win, qag, wqb, kvag, wkvb, qg, kg, sqg, skg, rc)


def _mla_attn_kernel(q_ref, k_ref, v_ref, o_ref, *, tq):
    qi = pl.program_id(2)
    q = q_ref[0, 0]

    def chunk(j, carry, masked):
        m, l, acc = carry
        start = pl.multiple_of(j * tq, tq)
        k = k_ref[0, 0, pl.ds(start, tq), :]
        v = v_ref[0, 0, pl.ds(start, tq), :]
        s = _dot_nt(q, k)
        if masked:
            row = lax.broadcasted_iota(jnp.int32, (tq, tq), 0)
            col = lax.broadcasted_iota(jnp.int32, (tq, tq), 1)
            s = jnp.where(col <= row, s, NEG_INF)
        m_new = jnp.maximum(m, jnp.max(s, axis=-1, keepdims=True))
        alpha = jnp.exp(m - m_new)
        p = jnp.exp(s - m_new)
        l = alpha * l + jnp.sum(p, axis=-1, keepdims=True)
        acc = alpha * acc + _dot(p.astype(BF16), v)
        return m_new, l, acc

    init = (jnp.full((tq, 1), NEG_INF, F32), jnp.zeros((tq, 1), F32), jnp.zeros((tq, MLA_V), F32))
    carry = lax.fori_loop(0, qi, lambda j, c: chunk(j, c, False), init)
    m, l, acc = chunk(qi, carry, True)
    o_ref[0] = acc / l


def _mla_attn(qm, km, vm, *, tq):
    B, H, S, _ = qm.shape
    return pl.pallas_call(
        functools.partial(_mla_attn_kernel, tq=tq),
        grid=(B, H, S // tq),
        in_specs=[
            pl.BlockSpec((1, 1, tq, MLA_QK_PAD), lambda b, h, i: (b, h, i, 0)),
            pl.BlockSpec((1, 1, S, MLA_QK_PAD), lambda b, h, i: (b, h, 0, 0)),
            pl.BlockSpec((1, 1, S, MLA_V), lambda b, h, i: (b, h, 0, 0)),
        ],
        out_specs=pl.BlockSpec((1, tq, MLA_V), lambda b, h, i: (b, i, h)),
        out_shape=jax.ShapeDtypeStruct((B, S, H * MLA_V), F32),
        compiler_params=pltpu.CompilerParams(dimension_semantics=("parallel", "parallel", "arbitrary"),
                                             vmem_limit_bytes=VMEM_LIMIT),
        name="mla_attn",
    )(qm, km, vm)


def _t5_bucket(dist):
    n = jnp.maximum(dist, 0)
    max_exact = N_BUCKETS // 2
    nf = jnp.maximum(n, 1).astype(F32)
    large = max_exact + (jnp.log(nf / max_exact) / math.log(MAX_DISTANCE / max_exact)
                         * (N_BUCKETS - max_exact)).astype(jnp.int32)
    large = jnp.minimum(large, N_BUCKETS - 1)
    return jnp.where(n < max_exact, n, large)


def _swa_attn_kernel(sink_ref, q_ref, kp_ref, kc_ref, vp_ref, vc_ref, posq_ref, pkp_ref, pkc_ref, tab_ref, o_ref):
    n = pl.program_id(1)
    kb = jnp.concatenate([kp_ref[0], kc_ref[0]], axis=0)
    vb = jnp.concatenate([vp_ref[0], vc_ref[0]], axis=0)
    kpos = jnp.concatenate([pkp_ref[0], pkc_ref[0]], axis=1)
    bucket = _t5_bucket(posq_ref[0] - kpos)

    row = lax.broadcasted_iota(jnp.int32, (BLOCK, 2 * BLOCK), 0)
    col = lax.broadcasted_iota(jnp.int32, (BLOCK, 2 * BLOCK), 1)
    off = row + BLOCK - col
    valid = (off >= 0) & (off < WINDOW) & ((col >= BLOCK) | (n > 0))

    lane = lax.broadcasted_iota(jnp.int32, (1, LANES), 1)
    lo = lane < SWA_HD
    zero = jnp.zeros((), BF16)

    outs = []
    for pair in range(SWA_HEADS // 2):
        g = (2 * pair) // SWA_GROUP
        qp = q_ref[0, :, pair * LANES:(pair + 1) * LANES]
        kg = kb[:, g * LANES:(g + 1) * LANES]
        vg = vb[:, g * LANES:(g + 1) * LANES]
        o_pair = None
        for half in range(2):
            h = 2 * pair + half
            sel = lo if half == 0 else jnp.logical_not(lo)
            s = _dot_nt(qp, jnp.where(sel, kg, zero))
            tab_row = jnp.broadcast_to(tab_ref[h:h + 1, :], (BLOCK, LANES))
            bias = jnp.concatenate(
                [jnp.take_along_axis(tab_row, bucket[:, c * LANES:(c + 1) * LANES], axis=1) for c in range(2)], axis=1)
            s = jnp.where(valid, s + bias, NEG_INF)
            sink = sink_ref[h]
            m = jnp.maximum(jnp.max(s, axis=-1, keepdims=True), sink)
            p = jnp.exp(s - m)
            denom = jnp.sum(p, axis=-1, keepdims=True) + jnp.exp(sink - m)
            p = (p / denom).astype(BF16)
            o = _dot(p, jnp.where(sel, vg, zero))
            o_pair = o if o_pair is None else o_pair + o
        outs.append(o_pair)
    o_ref[0] = jnp.concatenate(outs, axis=1)


def _swa_attn(sinks, qs, ks, vs, pos_col, pos_row, tab):
    B, S, _ = qs.shape
    nb = S // BLOCK
    cur = lambda w: pl.BlockSpec((1, BLOCK, w), lambda b, n: (b, n, 0))
    prev = lambda w: pl.BlockSpec((1, BLOCK, w), lambda b, n: (b, jnp.maximum(n - 1, 0), 0))
    return pl.pallas_call(
        _swa_attn_kernel,
        grid=(B, nb),
        in_specs=[
            pl.BlockSpec(memory_space=pltpu.SMEM),
            cur(SWA_HEADS * SWA_HD), prev(2 * LANES), cur(2 * LANES), prev(2 * LANES), cur(2 * LANES),
            cur(1),
            pl.BlockSpec((1, 1, BLOCK), lambda b, n: (b, 0, jnp.maximum(n - 1, 0))),
            pl.BlockSpec((1, 1, BLOCK), lambda b, n: (b, 0, n)),
            _resident(tab.shape),
        ],
        out_specs=cur(SWA_HEADS * SWA_HD),
        out_shape=jax.ShapeDtypeStruct((B, S, SWA_HEADS * SWA_HD), F32),
        compiler_params=pltpu.CompilerParams(dimension_semantics=("parallel", "arbitrary"), vmem_limit_bytes=VMEM_LIMIT),
        name="swa_attn",
    )(sinks, qs, ks, ks, vs, vs, pos_col, pos_row, pos_row, tab)


def _out_proj_kernel(om_ref, os_ref, x_ref, gout_ref, wout_ref, g2_ref, h_ref, n2_ref):
    om = om_ref[...]
    osw = os_ref[...]
    half = om.shape[-1]
    a = om * _rms_scale(om, half) * gout_ref[:, :half]
    b = osw * _rms_scale(osw, half) * gout_ref[:, half:]
    mixed = jnp.concatenate([a.astype(BF16), b.astype(BF16)], axis=1)
    h = x_ref[...] + _dot(mixed, wout_ref[...])
    h_ref[...] = h
    n2_ref[...] = (h * _rms_scale(h, h.shape[-1]) * g2_ref[...]).astype(BF16)


def _out_proj(om, osw, x2, gout, wout, g2, *, tm):
    T, D = x2.shape
    half = om.shape[-1]
    tok = lambda w: pl.BlockSpec((tm, w), lambda t: (t, 0))
    return pl.pallas_call(
        _out_proj_kernel,
        grid=(T // tm,),
        in_specs=[tok(half), tok(half), tok(D), _resident(gout.shape), _resident(wout.shape), _resident(g2.shape)],
        out_specs=[tok(D), tok(D)],
        out_shape=[jax.ShapeDtypeStruct((T, D), F32), jax.ShapeDtypeStruct((T, D), BF16)],
        compiler_params=pltpu.CompilerParams(dimension_semantics=("parallel",), vmem_limit_bytes=VMEM_LIMIT),
        name="out_proj",
    )(om, osw, x2, gout, wout, g2)


_CAND = [(a, b) for a in range(PEER_TOPK) for b in range(PEER_TOPK) if (a + 1) * (b + 1) <= PEER_TOPK]


def _peer_route_kernel(n2_ref, wqt_ref, sk_ref, cnt_ref, a_ref, r2_ref, b_ref,
                       s_scr, rank_scr, vals_scr, n_scr, m_scr, z_scr, *, tr):
    n_lt = tr // LANES
    n_hp = 2 * PEER_HEADS
    qt = lax.dot_general(wqt_ref[...], n2_ref[...], (((1,), (1,)), ((), ())),
                         preferred_element_type=F32).astype(BF16)
    for hp in range(n_hp):
        s_scr[hp] = _dot(sk_ref[hp], qt[hp * PEER_HALF:(hp + 1) * PEER_HALF, :])

    key = lax.broadcasted_iota(jnp.int32, (PEER_NKEYS, LANES), 0).astype(F32)
    head = lax.broadcasted_iota(jnp.int32, (PEER_HEADS, LANES), 0)
    vals_scr[...] = jnp.zeros_like(vals_scr)

    def top16(hp, _):
        h = hp // 2
        p = hp % 2
        for lt in range(n_lt):
            lanes = slice(lt * LANES, (lt + 1) * LANES)
            v = s_scr[hp, :, lanes]
            rank = jnp.full((PEER_NKEYS, LANES), float(PEER_TOPK), F32)
            for r in range(PEER_TOPK):
                m = jnp.max(v, axis=0, keepdims=True)
                first = jnp.min(jnp.where(v == m, key, float(PEER_NKEYS)), axis=0, keepdims=True)
                hit = key == first
                v = jnp.where(hit, -jnp.inf, v)
                rank = jnp.where(hit, float(r), rank)
                vals_scr[p, r, :, lanes] = jnp.where(head == h, m, vals_scr[p, r, :, lanes])
            rank_scr[hp, :, lanes] = rank
        return 0

    lax.fori_loop(0, n_hp, top16, 0)

    for lt in range(n_lt):
        lanes = slice(lt * LANES, (lt + 1) * LANES)
        v1 = [vals_scr[0, a, :, lanes] for a in range(PEER_TOPK)]
        v2 = [vals_scr[1, b, :, lanes] for b in range(PEER_TOPK)]
        c = [v1[a] + v2[b] for (a, b) in _CAND]
        flat = [float(a * PEER_TOPK + b) for (a, b) in _CAND]
        for _ in range(PEER_TOPK):
            m = functools.reduce(jnp.maximum, c)
            first = functools.reduce(jnp.minimum, [jnp.where(ci == m, fi, 1e9) for ci, fi in zip(c, flat)])
            c = [jnp.where(first == fi, -jnp.inf, ci) for ci, fi in zip(c, flat)]
        e1 = [jnp.exp(v1[a] - v1[0]) for a in range(PEER_TOPK)]
        e2 = [jnp.exp(v2[b] - v2[0]) for b in range(PEER_TOPK)]
        z = jnp.zeros_like(v1[0])
        n_a = [jnp.zeros_like(v1[0]) for _ in range(PEER_TOPK)]
        for ci, (a, b) in zip(c, _CAND):
            taken = ci == -jnp.inf
            z = z + jnp.where(taken, e1[a] * e2[b], 0.0)
            n_a[a] = n_a[a] + jnp.where(taken, 1.0, 0.0)
        for a in range(PEER_TOPK):
            n_scr[a, :, lanes] = n_a[a]
        m_scr[0, :, lanes] = v1[0]
        m_scr[1, :, lanes] = v2[0]
        z_scr[:, lanes] = 1.0 / z

    def spread(h, _):
        rank1 = rank_scr[2 * h]
        cnt = jnp.zeros((PEER_NKEYS, tr), F32)
        for a in range(PEER_TOPK):
            cnt = jnp.where(rank1 == float(a), n_scr[a, pl.ds(h, 1), :], cnt)
        cnt_ref[h] = cnt
        a_ref[h] = jnp.exp(s_scr[2 * h] - m_scr[0, pl.ds(h, 1), :]) * z_scr[pl.ds(h, 1), :]
        r2_ref[h] = rank_scr[2 * h + 1].astype(BF16)
        b_ref[h] = jnp.exp(s_scr[2 * h + 1] - m_scr[1, pl.ds(h, 1), :]).astype(BF16)
        return 0

    lax.fori_loop(0, PEER_HEADS, spread, 0)


def _peer_route(n2, wqt, sk, *, tr):
    T, D = n2.shape
    route = pl.BlockSpec((PEER_HEADS, PEER_NKEYS, tr), lambda t: (0, 0, t))
    shape = lambda dt: jax.ShapeDtypeStruct((PEER_HEADS, PEER_NKEYS, T), dt)
    return pl.pallas_call(
        functools.partial(_peer_route_kernel, tr=tr),
        grid=(T // tr,),
        in_specs=[pl.BlockSpec((tr, D), lambda t: (t, 0)), _resident(wqt.shape), _resident(sk.shape)],
        out_specs=[route] * 4,
        out_shape=[shape(F32), shape(F32), shape(BF16), shape(BF16)],
        scratch_shapes=[
            pltpu.VMEM((2 * PEER_HEADS, PEER_NKEYS, tr), F32),
            pltpu.VMEM((2 * PEER_HEADS, PEER_NKEYS, tr), F32),
            pltpu.VMEM((2, PEER_TOPK, PEER_HEADS, tr), F32),
            pltpu.VMEM((PEER_TOPK, PEER_HEADS, tr), F32),
            pltpu.VMEM((2, PEER_HEADS, tr), F32),
            pltpu.VMEM((PEER_HEADS, tr), F32),
        ],
        compiler_params=pltpu.CompilerParams(dimension_semantics=("parallel",), vmem_limit_bytes=VMEM_LIMIT),
        name="peer_route",
    )(n2, wqt, sk)


PACK = 16


def _peer_dense_kernel(n2_ref, u_ref, vt_ref, cnt_ref, a_ref, r2_ref, b_ref, h_ref, o_ref, acc_ref, act_scr, w_scr, *, te):
    e = pl.program_id(1)
    tm = n2_ref.shape[0]
    n_sub = te // PEER_NKEYS

    @pl.when(e == 0)
    def _():
        acc_ref[...] = jnp.zeros_like(acc_ref)

    act_scr[...] = _dot_nt(u_ref[...], n2_ref[...])
    zero = jnp.zeros((), BF16)
    for ii in range(n_sub):
        i = e * n_sub + ii
        cnt = [jnp.broadcast_to(cnt_ref[h, pl.ds(i, 1), :], (PACK, tm)).astype(BF16) for h in range(PEER_HEADS)]
        a = [jnp.broadcast_to(a_ref[h, pl.ds(i, 1), :], (PACK, tm)).astype(BF16) for h in range(PEER_HEADS)]
        for jg in range(PEER_NKEYS // PACK):
            keys = slice(jg * PACK, (jg + 1) * PACK)
            gate = None
            for h in range(PEER_HEADS):
                term = jnp.where(r2_ref[h, keys, :] < cnt[h], b_ref[h, keys, :], zero) * a[h]
                gate = term if gate is None else gate + term
            rows = slice(ii * PEER_NKEYS + jg * PACK, ii * PEER_NKEYS + (jg + 1) * PACK)
            w_scr[rows, :] = jax.nn.gelu(act_scr[rows, :]).astype(BF16) * gate
    acc_ref[...] += _dot(vt_ref[...], w_scr[...])

    @pl.when(e == pl.num_programs(1) - 1)
    def _():
        o_ref[...] = h_ref[...] + acc_ref[...].T


def _peer_dense(n2, u, vt, cnt, a, r2, b, h, *, tm, te):
    T, D = n2.shape
    N = u.shape[0]
    route = pl.BlockSpec((PEER_HEADS, PEER_NKEYS, tm), lambda t, e: (0, 0, t))
    return pl.pallas_call(
        functools.partial(_peer_dense_kernel, te=te),
        grid=(T // tm, N // te),
        in_specs=[
            pl.BlockSpec((tm, D), lambda t, e: (t, 0)),
            pl.BlockSpec((te, D), lambda t, e: (e, 0)),
            pl.BlockSpec((D, te), lambda t, e: (0, e)),
            route, route, route, route,
            pl.BlockSpec((tm, D), lambda t, e: (t, 0), pipeline_mode=pl.Buffered(1)),
        ],
        out_specs=pl.BlockSpec((tm, D), lambda t, e: (t, 0)),
        out_shape=jax.ShapeDtypeStruct((T, D), F32),
        scratch_shapes=[pltpu.VMEM((D, tm), F32), pltpu.VMEM((te, tm), F32), pltpu.VMEM((te, tm), BF16)],
        compiler_params=pltpu.CompilerParams(dimension_semantics=("parallel", "arbitrary"), vmem_limit_bytes=VMEM_LIMIT),
        name="peer_dense",
    )(n2, u, vt, cnt, a, r2, b, h)


def _row(v, width=None):
    v = v.astype(F32).reshape(1, -1)
    if width is not None and v.shape[1] < width:
        v = jnp.pad(v, ((0, 0), (0, width - v.shape[1])))
    return v


def _rope_consts():
    half = MLA_ROPE // 2
    inv_freq = ROPE_THETA ** (-jnp.arange(half, dtype=F32) / half)
    z = jnp.zeros((LANES - MLA_ROPE,), F32)
    rows = [
        jnp.concatenate([inv_freq, inv_freq, z]),
        jnp.concatenate([jnp.ones((MLA_ROPE,), F32), z]),
        jnp.concatenate([-jnp.ones((half,), F32), jnp.ones((half,), F32), z]),
    ]
    return jnp.pad(jnp.stack(rows), ((0, 5), (0, 0)))


def _layer(x, positions, norm1_gain, w_in, q_a_gain, w_q_b, kv_a_gain, w_kv_b, mla_q_gain, mla_k_gain,
           swa_q_gain, swa_k_gain, swa_sinks, rel_bias_table, group_out_gain, w_out, norm2_gain,
           peer_w_q, peer_sub_keys, peer_u, peer_v, *, tm_in, tq, tm_out, tr, tm_peer, te):
    B, S, D = x.shape
    T = B * S

    zpad = jnp.zeros((D, LANES - MLA_ROPE), w_in.dtype)
    o = np.cumsum((MLA_Q_RANK, MLA_KV_RANK, MLA_ROPE, SWA_HEADS * SWA_HD, SWA_KV_HEADS * SWA_HD)).tolist()
    k_swa, v_swa = w_in[:, o[3]:o[4]], w_in[:, o[4]:]
    dup = lambda w: jnp.concatenate([w[:, :SWA_HD], w[:, :SWA_HD], w[:, SWA_HD:], w[:, SWA_HD:]], axis=1)
    win = jnp.concatenate([w_in[:, :o[2]], zpad, w_in[:, o[2]:o[3]], dup(k_swa), dup(v_swa)], axis=1).astype(BF16)
    wqb = jnp.pad(w_q_b.reshape(MLA_Q_RANK, MLA_HEADS, MLA_QK), ((0, 0), (0, 0), (0, MLA_QK_PAD - MLA_QK)))
    wqb = wqb.reshape(MLA_Q_RANK, MLA_HEADS * MLA_QK_PAD).astype(BF16)

    qm, km, vm, qs, ks, vs = _in_proj(
        x, positions.reshape(B, S, 1), _row(norm1_gain), win, _row(q_a_gain), wqb, _row(kv_a_gain),
        w_kv_b.astype(BF16), _row(mla_q_gain, MLA_QK_PAD), _row(mla_k_gain, MLA_QK_PAD),
        _row(jnp.tile(swa_q_gain, 2)), _row(jnp.tile(swa_k_gain, 2)), _rope_consts(), tm=tm_in)

    o_mla = _mla_attn(qm, km, vm, tq=tq)
    tab = jnp.pad(rel_bias_table.astype(F32).T, ((0, 0), (0, LANES - N_BUCKETS)))
    o_swa = _swa_attn(swa_sinks.astype(F32), qs, ks, vs, positions.reshape(B, S, 1), positions.reshape(B, 1, S), tab)

    h, n2 = _out_proj(o_mla.reshape(T, -1), o_swa.reshape(T, -1), x.reshape(T, D), _row(group_out_gain),
                      w_out.astype(BF16), _row(norm2_gain), tm=tm_out)

    sk = peer_sub_keys.reshape(2 * PEER_HEADS, PEER_NKEYS, PEER_HALF).astype(BF16)
    cnt, a, r2, b = _peer_route(n2, peer_w_q.T.astype(BF16), sk, tr=tr)
    out = _peer_dense(n2, peer_u.astype(BF16), peer_v.T.astype(BF16), cnt, a, r2, b, h, tm=tm_peer, te=te)
    return out.reshape(B, S, D)


def kernel(x, positions, norm1_gain, w_in, q_a_gain, w_q_b, kv_a_gain, w_kv_b, mla_q_gain, mla_k_gain, swa_q_gain, swa_k_gain, swa_sinks, rel_bias_table, group_out_gain, w_out, norm2_gain, peer_w_q, peer_sub_keys, peer_u, peer_v):
    assert norm1_gain.shape[0] == 1, "single-layer trunk"
    return _layer(x, positions, norm1_gain[0], w_in[0], q_a_gain[0], w_q_b[0], kv_a_gain[0], w_kv_b[0],
                  mla_q_gain[0], mla_k_gain[0], swa_q_gain[0], swa_k_gain[0], swa_sinks[0], rel_bias_table,
                  group_out_gain[0], w_out[0], norm2_gain[0], peer_w_q[0], peer_sub_keys[0], peer_u[0], peer_v[0],
                  tm_in=256, tq=512, tm_out=512, tr=256, tm_peer=512, te=1024)
```

```python
import functools
import math

import jax
import jax.numpy as jnp
import numpy as np
from jax import lax
from jax.experimental import pallas as pl
from jax.experimental.pallas import tpu as pltpu

EPS = 1e-6
NEG_INF = -1e30
LANES = 128
VMEM_LIMIT = 56 << 20

MLA_HEADS = 8
MLA_NOPE = 128
MLA_ROPE = 64
MLA_V = 128
MLA_QK = MLA_NOPE + MLA_ROPE
MLA_QK_PAD = 256
MLA_Q_RANK = 512
MLA_KV_RANK = 256
ROPE_THETA = 10000.0

SWA_HEADS = 16
SWA_KV_HEADS = 2
SWA_HD = 64
SWA_GROUP = SWA_HEADS // SWA_KV_HEADS
WINDOW = 128
BLOCK = 128
N_BUCKETS = 32
MAX_DISTANCE = 128

PEER_HEADS = 8
PEER_NKEYS = 128
PEER_TOPK = 16
PEER_HALF = 128

BF16 = jnp.bfloat16
F32 = jnp.float32


def _resident(shape):
    nd = len(shape)
    return pl.BlockSpec(shape, lambda *_: (0,) * nd, pipeline_mode=pl.Buffered(1))


def _rms_scale(x, width):
    return lax.rsqrt(jnp.sum(x * x, axis=-1, keepdims=True) * (1.0 / width) + EPS)


def _dot(a, b):
    return jnp.dot(a, b, preferred_element_type=F32)


def _dot_nt(a, b):
    return lax.dot_general(a, b, (((1,), (1,)), ((), ())), preferred_element_type=F32)


_C_QLAT = 0
_C_KVLAT = _C_QLAT + MLA_Q_RANK
_C_KPE = _C_KVLAT + MLA_KV_RANK
_C_QSWA = _C_KPE + LANES
_C_KSWA = _C_QSWA + SWA_HEADS * SWA_HD
_C_VSWA = _C_KSWA + 2 * LANES
_C_END = _C_VSWA + 2 * LANES


def _rope(x, cos_t, sin_t):
    partner = pltpu.roll(x, 32, axis=1) + pltpu.roll(x, 96, axis=1)
    return x * cos_t + partner * sin_t


def _in_proj_kernel(x_ref, pos_ref, g1_ref, win_ref, qag_ref, wqb_ref, kvag_ref, wkn_ref, wvt_ref,
                    qg_ref, kg_ref, sqg_ref, skg_ref, rc_ref,
                    qm_ref, km_ref, vm_ref, qs_ref, ks_ref, vs_ref):
    x = x_ref[0]
    n1 = x * _rms_scale(x, x.shape[-1]) * g1_ref[...]
    proj = _dot(n1.astype(BF16), win_ref[...])

    pos = pos_ref[0].astype(F32)
    ang = pos * rc_ref[0:1, :]
    cos_t = jnp.cos(ang) * rc_ref[1:2, :]
    sin_t = jnp.sin(ang) * rc_ref[2:3, :]

    q_lat = proj[:, _C_QLAT:_C_QLAT + MLA_Q_RANK]
    ql = q_lat * _rms_scale(q_lat, MLA_Q_RANK) * qag_ref[...]
    q = _dot(ql.astype(BF16), wqb_ref[...])
    q_scale = MLA_QK ** -0.5 * math.log2(math.e)
    for h in range(MLA_HEADS):
        qh = q[:, h * MLA_QK_PAD:(h + 1) * MLA_QK_PAD]
        qn = qh * _rms_scale(qh, MLA_QK) * qg_ref[...]
        qr = _rope(qn[:, LANES:], cos_t, sin_t)
        qm_ref[0, h, :, 0:LANES] = (qn[:, :LANES] * q_scale).astype(BF16)
        qm_ref[0, h, :, LANES:] = (qr * q_scale).astype(BF16)

    kv_lat = proj[:, _C_KVLAT:_C_KVLAT + MLA_KV_RANK]
    kvl = kv_lat * _rms_scale(kv_lat, MLA_KV_RANK) * kvag_ref[...]
    kvl = kvl.astype(BF16)
    k_nope = _dot(kvl, wkn_ref[...])
    kpe = proj[:, _C_KPE:_C_KPE + LANES]
    kpe_ss = jnp.sum(kpe * kpe, axis=-1, keepdims=True)
    kr = _rope(kpe * kg_ref[:, LANES:], cos_t, sin_t)
    for h in range(MLA_HEADS):
        kn = k_nope[:, h * MLA_NOPE:(h + 1) * MLA_NOPE]
        ss = jnp.sum(kn * kn, axis=-1, keepdims=True) + kpe_ss
        r = lax.rsqrt(ss * (1.0 / MLA_QK) + EPS)
        km_ref[0, h, :, 0:LANES] = (kn * r * kg_ref[:, :LANES]).astype(BF16)
        km_ref[0, h, :, LANES:] = (kr * r).astype(BF16)
        vm_ref[0, h, 0] = _dot_nt(wvt_ref[h], kvl).astype(BF16)

    lane = lax.broadcasted_iota(jnp.int32, (1, LANES), 1)
    lo = lane < SWA_HD
    s_scale = SWA_HD ** -0.5
    for p in range(SWA_HEADS // 2):
        v = proj[:, _C_QSWA + p * LANES:_C_QSWA + (p + 1) * LANES]
        sq = v * v
        ss_lo = jnp.sum(jnp.where(lo, sq, 0.0), axis=-1, keepdims=True)
        ss_hi = jnp.sum(jnp.where(lo, 0.0, sq), axis=-1, keepdims=True)
        r = jnp.where(lo, lax.rsqrt(ss_lo * (1.0 / SWA_HD) + EPS), lax.rsqrt(ss_hi * (1.0 / SWA_HD) + EPS))
        qs_ref[0, :, p * LANES:(p + 1) * LANES] = (v * r * sqg_ref[...] * s_scale).astype(BF16)
    for g in range(SWA_KV_HEADS):
        v = proj[:, _C_KSWA + g * LANES:_C_KSWA + (g + 1) * LANES]
        r = lax.rsqrt(jnp.sum(v * v, axis=-1, keepdims=True) * (0.5 / SWA_HD) + EPS)
        ks_ref[0, :, g * LANES:(g + 1) * LANES] = (v * r * skg_ref[...]).astype(BF16)
    vs_ref[0] = proj[:, _C_VSWA:_C_END].astype(BF16)


def _in_proj(x, pos_col, g1, win, qag, wqb, kvag, wkn, wvt, qg, kg, sqg, skg, rc, *, tm, tk):
    B, S, D = x.shape
    grid = (B, S // tm)
    per_chunk = tk // tm
    tok = lambda w: pl.BlockSpec((1, tm, w), lambda b, s: (b, s, 0))
    heads = lambda w: pl.BlockSpec((1, MLA_HEADS, tm, w), lambda b, s: (b, 0, s, 0))
    vt_spec = pl.BlockSpec((1, MLA_HEADS, 1, MLA_V, tm), lambda b, s: (b, 0, s // per_chunk, 0, s % per_chunk))
    return pl.pallas_call(
        _in_proj_kernel,
        grid=grid,
        in_specs=[tok(D), tok(1)] + [_resident(a.shape) for a in (g1, win, qag, wqb, kvag, wkn, wvt, qg, kg, sqg, skg, rc)],
        out_specs=[heads(MLA_QK_PAD), heads(MLA_QK_PAD), vt_spec, tok(SWA_HEADS * SWA_HD), tok(2 * LANES), tok(2 * LANES)],
        out_shape=[
            jax.ShapeDtypeStruct((B, MLA_HEADS, S, MLA_QK_PAD), BF16),
            jax.ShapeDtypeStruct((B, MLA_HEADS, S, MLA_QK_PAD), BF16),
            jax.ShapeDtypeStruct((B, MLA_HEADS, S // tk, MLA_V, tk), BF16),
            jax.ShapeDtypeStruct((B, S, SWA_HEADS * SWA_HD), BF16),
            jax.ShapeDtypeStruct((B, S, 2 * LANES), BF16),
            jax.ShapeDtypeStruct((B, S, 2 * LANES), BF16),
        ],
        compiler_params=pltpu.CompilerParams(dimension_semantics=("parallel", "parallel"), vmem_limit_bytes=VMEM_LIMIT),
        name="in_proj",
    )(x, pos_col, g1, win, qag, wqb, kvag, wkn, wvt, qg, kg, sqg, skg, rc)


def _mla_attn_kernel(q_ref, k_ref, vt_ref, o_ref, *, tq):
    qi = pl.program_id(2)
    heads = q_ref.shape[1]

    def chunk(j, carry, masked):
        start = pl.multiple_of(j * tq, tq)
        out = []
        scores = [_dot_nt(k_ref[0, hh, pl.ds(start, tq), :], q_ref[0, hh]) for hh in range(heads)]
        for hh in range(heads):
            m, l, acc = carry[hh]
            st = scores[hh]
            if masked:
                key = lax.broadcasted_iota(jnp.int32, (tq, tq), 0)
                qry = lax.broadcasted_iota(jnp.int32, (tq, tq), 1)
                st = jnp.where(key <= qry, st, NEG_INF)
            m_new = jnp.maximum(m, jnp.max(st, axis=0, keepdims=True))
            alpha = jnp.exp2(m - m_new)
            p = jnp.exp2(st - m_new)
            l = alpha * l + jnp.sum(p, axis=0, keepdims=True)
            acc = alpha * acc + _dot(vt_ref[0, hh, j], p.astype(BF16))
            out.append((m_new, l, acc))
        return tuple(out)

    init = tuple((jnp.full((1, tq), NEG_INF, F32), jnp.zeros((1, tq), F32), jnp.zeros((MLA_V, tq), F32))
                 for _ in range(heads))
    carry = lax.fori_loop(0, qi, lambda j, c: chunk(j, c, False), init)
    for hh, (m, l, acc) in enumerate(chunk(qi, carry, True)):
        o_ref[0, :, hh * MLA_V:(hh + 1) * MLA_V] = (acc / l).T


def _mla_attn(qm, km, vm, *, tq, hb):
    B, H, S, _ = qm.shape
    return pl.pallas_call(
        functools.partial(_mla_attn_kernel, tq=tq),
        grid=(B, H // hb, S // tq),
        in_specs=[
            pl.BlockSpec((1, hb, tq, MLA_QK_PAD), lambda b, h, i: (b, h, i, 0)),
            pl.BlockSpec((1, hb, S, MLA_QK_PAD), lambda b, h, i: (b, h, 0, 0)),
            pl.BlockSpec((1, hb, S // tq, MLA_V, tq), lambda b, h, i: (b, h, 0, 0, 0)),
        ],
        out_specs=pl.BlockSpec((1, tq, hb * MLA_V), lambda b, h, i: (b, i, h)),
        out_shape=jax.ShapeDtypeStruct((B, S, H * MLA_V), F32),
        compiler_params=pltpu.CompilerParams(dimension_semantics=("parallel", "parallel", "arbitrary"),
                                             vmem_limit_bytes=VMEM_LIMIT),
        name="mla_attn",
    )(qm, km, vm)


def _t5_bucket(dist):
    n = jnp.maximum(dist, 0)
    max_exact = N_BUCKETS // 2
    nf = jnp.maximum(n, 1).astype(F32)
    large = max_exact + (jnp.log(nf / max_exact) / math.log(MAX_DISTANCE / max_exact)
                         * (N_BUCKETS - max_exact)).astype(jnp.int32)
    large = jnp.minimum(large, N_BUCKETS - 1)
    return jnp.where(n < max_exact, n, large)


def _swa_attn_kernel(sink_ref, q_ref, kp_ref, kc_ref, vp_ref, vc_ref, posq_ref, pkp_ref, pkc_ref, tab_ref, o_ref):
    n = pl.program_id(1)
    kb = jnp.concatenate([kp_ref[0], kc_ref[0]], axis=0)
    vb = jnp.concatenate([vp_ref[0], vc_ref[0]], axis=0)
    kpos = jnp.concatenate([pkp_ref[0], pkc_ref[0]], axis=1)
    bucket = _t5_bucket(posq_ref[0] - kpos)

    row = lax.broadcasted_iota(jnp.int32, (BLOCK, 2 * BLOCK), 0)
    col = lax.broadcasted_iota(jnp.int32, (BLOCK, 2 * BLOCK), 1)
    off = row + BLOCK - col
    valid = (off >= 0) & (off < WINDOW) & ((col >= BLOCK) | (n > 0))

    lane = lax.broadcasted_iota(jnp.int32, (1, LANES), 1)
    lo = lane < SWA_HD
    zero = jnp.zeros((), BF16)

    halves = (lo, jnp.logical_not(lo))
    k_half = [[jnp.where(sel, kb[:, g * LANES:(g + 1) * LANES], zero) for sel in halves] for g in range(SWA_KV_HEADS)]
    v_half = [[jnp.where(sel, vb[:, g * LANES:(g + 1) * LANES], zero) for sel in halves] for g in range(SWA_KV_HEADS)]
    scores = []
    for h in range(SWA_HEADS):
        qp = q_ref[0, :, (h // 2) * LANES:(h // 2 + 1) * LANES]
        scores.append(_dot_nt(qp, k_half[h // SWA_GROUP][h % 2]))

    outs = []
    for pair in range(SWA_HEADS // 2):
        g = (2 * pair) // SWA_GROUP
        o_pair = None
        for half in range(2):
            h = 2 * pair + half
            s = scores[h]
            tab_row = jnp.broadcast_to(tab_ref[h:h + 1, :], (BLOCK, LANES))
            bias = jnp.concatenate(
                [jnp.take_along_axis(tab_row, bucket[:, c * LANES:(c + 1) * LANES], axis=1) for c in range(2)], axis=1)
            s = jnp.where(valid, s + bias, NEG_INF)
            sink = sink_ref[h]
            m = jnp.maximum(jnp.max(s, axis=-1, keepdims=True), sink)
            p = jnp.exp(s - m)
            denom = jnp.sum(p, axis=-1, keepdims=True) + jnp.exp(sink - m)
            p = (p / denom).astype(BF16)
            o = _dot(p, v_half[g][half])
            o_pair = o if o_pair is None else o_pair + o
        outs.append(o_pair)
    o_ref[0] = jnp.concatenate(outs, axis=1)


def _swa_attn(sinks, qs, ks, vs, pos_col, pos_row, tab):
    B, S, _ = qs.shape
    nb = S // BLOCK
    cur = lambda w: pl.BlockSpec((1, BLOCK, w), lambda b, n: (b, n, 0))
    prev = lambda w: pl.BlockSpec((1, BLOCK, w), lambda b, n: (b, jnp.maximum(n - 1, 0), 0))
    return pl.pallas_call(
        _swa_attn_kernel,
        grid=(B, nb),
        in_specs=[
            pl.BlockSpec(memory_space=pltpu.SMEM),
            cur(SWA_HEADS * SWA_HD), prev(2 * LANES), cur(2 * LANES), prev(2 * LANES), cur(2 * LANES),
            cur(1),
            pl.BlockSpec((1, 1, BLOCK), lambda b, n: (b, 0, jnp.maximum(n - 1, 0))),
            pl.BlockSpec((1, 1, BLOCK), lambda b, n: (b, 0, n)),
            _resident(tab.shape),
        ],
        out_specs=cur(SWA_HEADS * SWA_HD),
        out_shape=jax.ShapeDtypeStruct((B, S, SWA_HEADS * SWA_HD), F32),
        compiler_params=pltpu.CompilerParams(dimension_semantics=("parallel", "arbitrary"), vmem_limit_bytes=VMEM_LIMIT),
        name="swa_attn",
    )(sinks, qs, ks, ks, vs, vs, pos_col, pos_row, pos_row, tab)


def _out_proj_kernel(om_ref, os_ref, x_ref, gout_ref, wout_ref, g2_ref, h_ref, n2_ref):
    om = om_ref[...]
    osw = os_ref[...]
    half = om.shape[-1]
    a = om * _rms_scale(om, half) * gout_ref[:, :half]
    b = osw * _rms_scale(osw, half) * gout_ref[:, half:]
    mixed = jnp.concatenate([a.astype(BF16), b.astype(BF16)], axis=1)
    h = x_ref[...] + _dot(mixed, wout_ref[...])
    h_ref[...] = h
    n2_ref[...] = (h * _rms_scale(h, h.shape[-1]) * g2_ref[...]).astype(BF16)


def _out_proj(om, osw, x2, gout, wout, g2, *, tm):
    T, D = x2.shape
    half = om.shape[-1]
    tok = lambda w: pl.BlockSpec((tm, w), lambda t: (t, 0))
    return pl.pallas_call(
        _out_proj_kernel,
        grid=(T // tm,),
        in_specs=[tok(half), tok(half), tok(D), _resident(gout.shape), _resident(wout.shape), _resident(g2.shape)],
        out_specs=[tok(D), tok(D)],
        out_shape=[jax.ShapeDtypeStruct((T, D), F32), jax.ShapeDtypeStruct((T, D), BF16)],
        compiler_params=pltpu.CompilerParams(dimension_semantics=("parallel",), vmem_limit_bytes=VMEM_LIMIT),
        name="out_proj",
    )(om, osw, x2, gout, wout, g2)


_CAND = [(a, b) for a in range(PEER_TOPK) for b in range(PEER_TOPK) if (a + 1) * (b + 1) <= PEER_TOPK]


def _peer_route_kernel(n2_ref, wqt_ref, sk_ref, cnt_ref, a_ref, r2_ref, b_ref,
                       s_scr, rank_scr, vals_scr, n_scr, m_scr, z_scr, *, tr):
    n_lt = tr // LANES
    n_hp = 2 * PEER_HEADS
    qt = lax.dot_general(wqt_ref[...], n2_ref[...], (((1,), (1,)), ((), ())),
                         preferred_element_type=F32).astype(BF16)
    for hp in range(n_hp):
        s_scr[hp] = _dot(sk_ref[hp], qt[hp * PEER_HALF:(hp + 1) * PEER_HALF, :])

    key = lax.broadcasted_iota(jnp.int32, (PEER_NKEYS, LANES), 0).astype(F32)
    head = lax.broadcasted_iota(jnp.int32, (PEER_HEADS, LANES), 0)
    vals_scr[...] = jnp.zeros_like(vals_scr)

    def top16(hp, _):
        h = hp // 2
        p = hp % 2
        for lt in range(n_lt):
            lanes = slice(lt * LANES, (lt + 1) * LANES)
            v = s_scr[hp, :, lanes]
            rank = jnp.full((PEER_NKEYS, LANES), float(PEER_TOPK), F32)
            for r in range(PEER_TOPK):
                m = jnp.max(v, axis=0, keepdims=True)
                first = jnp.min(jnp.where(v == m, key, float(PEER_NKEYS)), axis=0, keepdims=True)
                hit = key == first
                v = jnp.where(hit, -jnp.inf, v)
                rank = jnp.where(hit, float(r), rank)
                vals_scr[p, r, :, lanes] = jnp.where(head == h, m, vals_scr[p, r, :, lanes])
            rank_scr[hp, :, lanes] = rank
        return 0

    lax.fori_loop(0, n_hp, top16, 0)

    for lt in range(n_lt):
        lanes = slice(lt * LANES, (lt + 1) * LANES)
        v1 = [vals_scr[0, a, :, lanes] for a in range(PEER_TOPK)]
        v2 = [vals_scr[1, b, :, lanes] for b in range(PEER_TOPK)]
        c = [v1[a] + v2[b] for (a, b) in _CAND]
        flat = [float(a * PEER_TOPK + b) for (a, b) in _CAND]
        for _ in range(PEER_TOPK):
            m = functools.reduce(jnp.maximum, c)
            first = functools.reduce(jnp.minimum, [jnp.where(ci == m, fi, 1e9) for ci, fi in zip(c, flat)])
            c = [jnp.where(first == fi, -jnp.inf, ci) for ci, fi in zip(c, flat)]
        e1 = [jnp.exp(v1[a] - v1[0]) for a in range(PEER_TOPK)]
        e2 = [jnp.exp(v2[b] - v2[0]) for b in range(PEER_TOPK)]
        z = jnp.zeros_like(v1[0])
        n_a = [jnp.zeros_like(v1[0]) for _ in range(PEER_TOPK)]
        for ci, (a, b) in zip(c, _CAND):
            taken = ci == -jnp.inf
            z = z + jnp.where(taken, e1[a] * e2[b], 0.0)
            n_a[a] = n_a[a] + jnp.where(taken, 1.0, 0.0)
        for a in range(PEER_TOPK):
            n_scr[a, :, lanes] = n_a[a]
        m_scr[0, :, lanes] = v1[0]
        m_scr[1, :, lanes] = v2[0]
        z_scr[:, lanes] = 1.0 / z

    def spread(h, _):
        rank1 = rank_scr[2 * h]
        cnt = jnp.zeros((PEER_NKEYS, tr), F32)
        for a in range(PEER_TOPK):
            cnt = jnp.where(rank1 == float(a), n_scr[a, pl.ds(h, 1), :], cnt)
        cnt_ref[h] = cnt
        a_ref[h] = jnp.exp(s_scr[2 * h] - m_scr[0, pl.ds(h, 1), :]) * z_scr[pl.ds(h, 1), :]
        r2_ref[h] = rank_scr[2 * h + 1].astype(BF16)
        b_ref[h] = jnp.exp(s_scr[2 * h + 1] - m_scr[1, pl.ds(h, 1), :]).astype(BF16)
        return 0

    lax.fori_loop(0, PEER_HEADS, spread, 0)


def _peer_route(n2, wqt, sk, *, tr):
    T, D = n2.shape
    route = pl.BlockSpec((PEER_HEADS, PEER_NKEYS, tr), lambda t: (0, 0, t))
    shape = lambda dt: jax.ShapeDtypeStruct((PEER_HEADS, PEER_NKEYS, T), dt)
    return pl.pallas_call(
        functools.partial(_peer_route_kernel, tr=tr),
        grid=(T // tr,),
        in_specs=[pl.BlockSpec((tr, D), lambda t: (t, 0)), _resident(wqt.shape), _resident(sk.shape)],
        out_specs=[route] * 4,
        out_shape=[shape(F32), shape(F32), shape(BF16), shape(BF16)],
        scratch_shapes=[
            pltpu.VMEM((2 * PEER_HEADS, PEER_NKEYS, tr), F32),
            pltpu.VMEM((2 * PEER_HEADS, PEER_NKEYS, tr), F32),
            pltpu.VMEM((2, PEER_TOPK, PEER_HEADS, tr), F32),
            pltpu.VMEM((PEER_TOPK, PEER_HEADS, tr), F32),
            pltpu.VMEM((2, PEER_HEADS, tr), F32),
            pltpu.VMEM((PEER_HEADS, tr), F32),
        ],
        compiler_params=pltpu.CompilerParams(dimension_semantics=("parallel",), vmem_limit_bytes=VMEM_LIMIT),
        name="peer_route",
    )(n2, wqt, sk)


PACK = 16


def _peer_dense_kernel(n2_ref, u_ref, vt_ref, cnt_ref, a_ref, r2_ref, b_ref, h_ref, o_ref, acc_ref, act_scr, w_scr, *, te):
    e = pl.program_id(1)
    tm = n2_ref.shape[0]
    n_sub = te // PEER_NKEYS

    @pl.when(e == 0)
    def _():
        acc_ref[...] = jnp.zeros_like(acc_ref)

    act_scr[...] = _dot_nt(u_ref[...], n2_ref[...])
    zero = jnp.zeros((), BF16)
    for ii in range(n_sub):
        i = e * n_sub + ii
        cnt = [jnp.broadcast_to(cnt_ref[h, pl.ds(i, 1), :], (PACK, tm)).astype(BF16) for h in range(PEER_HEADS)]
        a = [jnp.broadcast_to(a_ref[h, pl.ds(i, 1), :], (PACK, tm)).astype(BF16) for h in range(PEER_HEADS)]
        for jg in range(PEER_NKEYS // PACK):
            keys = slice(jg * PACK, (jg + 1) * PACK)
            gate = None
            for h in range(PEER_HEADS):
                term = jnp.where(r2_ref[h, keys, :] < cnt[h], b_ref[h, keys, :], zero) * a[h]
                gate = term if gate is None else gate + term
            rows = slice(ii * PEER_NKEYS + jg * PACK, ii * PEER_NKEYS + (jg + 1) * PACK)
            w_scr[rows, :] = jax.nn.gelu(act_scr[rows, :]).astype(BF16) * gate
    acc_ref[...] += _dot(vt_ref[...], w_scr[...])

    @pl.when(e == pl.num_programs(1) - 1)
    def _():
        o_ref[...] = h_ref[...] + acc_ref[...].T


def _peer_dense(n2, u, vt, cnt, a, r2, b, h, *, tm, te):
    T, D = n2.shape
    N = u.shape[0]
    route = pl.BlockSpec((PEER_HEADS, PEER_NKEYS, tm), lambda t, e: (0, 0, t))
    return pl.pallas_call(
        functools.partial(_peer_dense_kernel, te=te),
        grid=(T // tm, N // te),
        in_specs=[
            pl.BlockSpec((tm, D), lambda t, e: (t, 0)),
            pl.BlockSpec((te, D), lambda t, e: (e, 0)),
            pl.BlockSpec((D, te), lambda t, e: (0, e)),
            route, route, route, route,
            pl.BlockSpec((tm, D), lambda t, e: (t, 0), pipeline_mode=pl.Buffered(1)),
        ],
        out_specs=pl.BlockSpec((tm, D), lambda t, e: (t, 0)),
        out_shape=jax.ShapeDtypeStruct((T, D), F32),
        scratch_shapes=[pltpu.VMEM((D, tm), F32), pltpu.VMEM((te, tm), F32), pltpu.VMEM((te, tm), BF16)],
        compiler_params=pltpu.CompilerParams(dimension_semantics=("parallel", "arbitrary"), vmem_limit_bytes=VMEM_LIMIT),
        name="peer_dense",
    )(n2, u, vt, cnt, a, r2, b, h)


def _row(v, width=None):
    v = v.astype(F32).reshape(1, -1)
    if width is not None and v.shape[1] < width:
        v = jnp.pad(v, ((0, 0), (0, width - v.shape[1])))
    return v


def _rope_consts():
    half = MLA_ROPE // 2
    inv_freq = ROPE_THETA ** (-jnp.arange(half, dtype=F32) / half)
    z = jnp.zeros((LANES - MLA_ROPE,), F32)
    rows = [
        jnp.concatenate([inv_freq, inv_freq, z]),
        jnp.concatenate([jnp.ones((MLA_ROPE,), F32), z]),
        jnp.concatenate([-jnp.ones((half,), F32), jnp.ones((half,), F32), z]),
    ]
    return jnp.pad(jnp.stack(rows), ((0, 5), (0, 0)))


def _layer(x, positions, norm1_gain, w_in, q_a_gain, w_q_b, kv_a_gain, w_kv_b, mla_q_gain, mla_k_gain,
           swa_q_gain, swa_k_gain, swa_sinks, rel_bias_table, group_out_gain, w_out, norm2_gain,
           peer_w_q, peer_sub_keys, peer_u, peer_v, *, tm_in, tq, tm_out, tr, tm_peer, te):
    B, S, D = x.shape
    T = B * S

    zpad = jnp.zeros((D, LANES - MLA_ROPE), w_in.dtype)
    o = np.cumsum((MLA_Q_RANK, MLA_KV_RANK, MLA_ROPE, SWA_HEADS * SWA_HD, SWA_KV_HEADS * SWA_HD)).tolist()
    k_swa, v_swa = w_in[:, o[3]:o[4]], w_in[:, o[4]:]
    dup = lambda w: jnp.concatenate([w[:, :SWA_HD], w[:, :SWA_HD], w[:, SWA_HD:], w[:, SWA_HD:]], axis=1)
    win = jnp.concatenate([w_in[:, :o[2]], zpad, w_in[:, o[2]:o[3]], dup(k_swa), dup(v_swa)], axis=1).astype(BF16)
    wqb = jnp.pad(w_q_b.reshape(MLA_Q_RANK, MLA_HEADS, MLA_QK), ((0, 0), (0, 0), (0, MLA_QK_PAD - MLA_QK)))
    wqb = wqb.reshape(MLA_Q_RANK, MLA_HEADS * MLA_QK_PAD).astype(BF16)

    wkv = w_kv_b.reshape(MLA_KV_RANK, MLA_HEADS, MLA_NOPE + MLA_V)
    wkn = wkv[:, :, :MLA_NOPE].reshape(MLA_KV_RANK, MLA_HEADS * MLA_NOPE).astype(BF16)
    wvt = wkv[:, :, MLA_NOPE:].transpose(1, 2, 0).astype(BF16)

    qm, km, vm, qs, ks, vs = _in_proj(
        x, positions.reshape(B, S, 1), _row(norm1_gain), win, _row(q_a_gain), wqb, _row(kv_a_gain),
        wkn, wvt, _row(mla_q_gain, MLA_QK_PAD), _row(mla_k_gain, MLA_QK_PAD),
        _row(jnp.tile(swa_q_gain, 2)), _row(jnp.tile(swa_k_gain, 2)), _rope_consts(), tm=tm_in, tk=tq)

    o_mla = _mla_attn(qm, km, vm, tq=tq, hb=4)
    tab = jnp.pad(rel_bias_table.astype(F32).T, ((0, 0), (0, LANES - N_BUCKETS)))
    o_swa = _swa_attn(swa_sinks.astype(F32), qs, ks, vs, positions.reshape(B, S, 1), positions.reshape(B, 1, S), tab)

    h, n2 = _out_proj(o_mla.reshape(T, -1), o_swa.reshape(T, -1), x.reshape(T, D), _row(group_out_gain),
                      w_out.astype(BF16), _row(norm2_gain), tm=tm_out)

    sk = peer_sub_keys.reshape(2 * PEER_HEADS, PEER_NKEYS, PEER_HALF).astype(BF16)
    cnt, a, r2, b = _peer_route(n2, peer_w_q.T.astype(BF16), sk, tr=tr)
    out = _peer_dense(n2, peer_u.astype(BF16), peer_v.T.astype(BF16), cnt, a, r2, b, h, tm=tm_peer, te=te)
    return out.reshape(B, S, D)


def kernel(x, positions, norm1_gain, w_in, q_a_gain, w_q_b, kv_a_gain, w_kv_b, mla_q_gain, mla_k_gain, swa_q_gain, swa_k_gain, swa_sinks, rel_bias_table, group_out_gain, w_out, norm2_gain, peer_w_q, peer_sub_keys, peer_u, peer_v):
    assert norm1_gain.shape[0] == 1, "single-layer trunk"
    return _layer(x, positions, norm1_gain[0], w_in[0], q_a_gain[0], w_q_b[0], kv_a_gain[0], w_kv_b[0],
                  mla_q_gain[0], mla_k_gain[0], swa_q_gain[0], swa_k_gain[0], swa_sinks[0], rel_bias_table,
                  group_out_gain[0], w_out[0], norm2_gain[0], peer_w_q[0], peer_sub_keys[0], peer_u[0], peer_v[0],
                  tm_in=256, tq=512, tm_out=512, tr=256, tm_peer=512, te=1024)
```

```python
import functools
import math

import jax
import jax.numpy as jnp
import numpy as np
from jax import lax
from jax.experimental import pallas as pl
from jax.experimental.pallas import tpu as pltpu

EPS = 1e-6
NEG_INF = -1e30
LANES = 128
VMEM_LIMIT = 56 << 20

MLA_HEADS = 8
MLA_NOPE = 128
MLA_ROPE = 64
MLA_V = 128
MLA_QK = MLA_NOPE + MLA_ROPE
MLA_QK_PAD = 256
MLA_Q_RANK = 512
MLA_KV_RANK = 256
ROPE_THETA = 10000.0

SWA_HEADS = 16
SWA_KV_HEADS = 2
SWA_HD = 64
SWA_GROUP = SWA_HEADS // SWA_KV_HEADS
WINDOW = 128
BLOCK = 128
N_BUCKETS = 32
MAX_DISTANCE = 128

PEER_HEADS = 8
PEER_NKEYS = 128
PEER_TOPK = 16
PEER_HALF = 128

BF16 = jnp.bfloat16
F32 = jnp.float32


def _resident(shape):
    nd = len(shape)
    return pl.BlockSpec(shape, lambda *_: (0,) * nd, pipeline_mode=pl.Buffered(1))


def _rms_scale(x, width):
    return lax.rsqrt(jnp.sum(x * x, axis=-1, keepdims=True) * (1.0 / width) + EPS)


def _dot(a, b):
    return jnp.dot(a, b, preferred_element_type=F32)


def _dot_nt(a, b):
    return lax.dot_general(a, b, (((1,), (1,)), ((), ())), preferred_element_type=F32)


_C_QLAT = 0
_C_KVLAT = _C_QLAT + MLA_Q_RANK
_C_KPE = _C_KVLAT + MLA_KV_RANK
_C_QSWA = _C_KPE + LANES
_C_KSWA = _C_QSWA + SWA_HEADS * SWA_HD
_C_VSWA = _C_KSWA + 2 * LANES
_C_END = _C_VSWA + 2 * LANES


def _rope(x, cos_t, sin_t):
    partner = pltpu.roll(x, 32, axis=1) + pltpu.roll(x, 96, axis=1)
    return x * cos_t + partner * sin_t


def _in_proj_kernel(x_ref, pos_ref, g1_ref, win_ref, qag_ref, wqb_ref, kvag_ref, wkn_ref, wvt_ref,
                    qg_ref, kg_ref, sqg_ref, skg_ref, rc_ref,
                    qm_ref, km_ref, vm_ref, qs_ref, ks_ref, vs_ref):
    x = x_ref[0]
    n1 = x * _rms_scale(x, x.shape[-1]) * g1_ref[...]
    proj = _dot(n1.astype(BF16), win_ref[...])

    pos = pos_ref[0].astype(F32)
    ang = pos * rc_ref[0:1, :]
    cos_t = jnp.cos(ang) * rc_ref[1:2, :]
    sin_t = jnp.sin(ang) * rc_ref[2:3, :]

    q_lat = proj[:, _C_QLAT:_C_QLAT + MLA_Q_RANK]
    ql = q_lat * _rms_scale(q_lat, MLA_Q_RANK) * qag_ref[...]
    q = _dot(ql.astype(BF16), wqb_ref[...])
    q_scale = MLA_QK ** -0.5 * math.log2(math.e)
    for h in range(MLA_HEADS):
        qh = q[:, h * MLA_QK_PAD:(h + 1) * MLA_QK_PAD]
        qn = qh * _rms_scale(qh, MLA_QK) * qg_ref[...]
        qr = _rope(qn[:, LANES:], cos_t, sin_t)
        qm_ref[0, h, :, 0:LANES] = (qn[:, :LANES] * q_scale).astype(BF16)
        qm_ref[0, h, :, LANES:] = (qr * q_scale).astype(BF16)

    kv_lat = proj[:, _C_KVLAT:_C_KVLAT + MLA_KV_RANK]
    kvl = kv_lat * _rms_scale(kv_lat, MLA_KV_RANK) * kvag_ref[...]
    kvl = kvl.astype(BF16)
    k_nope = _dot(kvl, wkn_ref[...])
    kpe = proj[:, _C_KPE:_C_KPE + LANES]
    kpe_ss = jnp.sum(kpe * kpe, axis=-1, keepdims=True)
    kr = _rope(kpe * kg_ref[:, LANES:], cos_t, sin_t)
    for h in range(MLA_HEADS):
        kn = k_nope[:, h * MLA_NOPE:(h + 1) * MLA_NOPE]
        ss = jnp.sum(kn * kn, axis=-1, keepdims=True) + kpe_ss
        r = lax.rsqrt(ss * (1.0 / MLA_QK) + EPS)
        km_ref[0, h, :, 0:LANES] = (kn * r * kg_ref[:, :LANES]).astype(BF16)
        km_ref[0, h, :, LANES:] = (kr * r).astype(BF16)
        vm_ref[0, h, 0] = _dot_nt(wvt_ref[h], kvl).astype(BF16)

    lane = lax.broadcasted_iota(jnp.int32, (1, LANES), 1)
    lo = lane < SWA_HD
    s_scale = SWA_HD ** -0.5
    for p in range(SWA_HEADS // 2):
        v = proj[:, _C_QSWA + p * LANES:_C_QSWA + (p + 1) * LANES]
        sq = v * v
        ss_lo = jnp.sum(jnp.where(lo, sq, 0.0), axis=-1, keepdims=True)
        ss_hi = jnp.sum(jnp.where(lo, 0.0, sq), axis=-1, keepdims=True)
        r = jnp.where(lo, lax.rsqrt(ss_lo * (1.0 / SWA_HD) + EPS), lax.rsqrt(ss_hi * (1.0 / SWA_HD) + EPS))
        qs_ref[0, :, p * LANES:(p + 1) * LANES] = (v * r * sqg_ref[...] * s_scale).astype(BF16)
    for g in range(SWA_KV_HEADS):
        v = proj[:, _C_KSWA + g * LANES:_C_KSWA + (g + 1) * LANES]
        r = lax.rsqrt(jnp.sum(v * v, axis=-1, keepdims=True) * (0.5 / SWA_HD) + EPS)
        ks_ref[0, :, g * LANES:(g + 1) * LANES] = (v * r * skg_ref[...]).astype(BF16)
    vs_ref[0] = proj[:, _C_VSWA:_C_END].astype(BF16)


def _in_proj(x, pos_col, g1, win, qag, wqb, kvag, wkn, wvt, qg, kg, sqg, skg, rc, *, tm, tk):
    B, S, D = x.shape
    grid = (B, S // tm)
    per_chunk = tk // tm
    tok = lambda w: pl.BlockSpec((1, tm, w), lambda b, s: (b, s, 0))
    heads = lambda w: pl.BlockSpec((1, MLA_HEADS, tm, w), lambda b, s: (b, 0, s, 0))
    vt_spec = pl.BlockSpec((1, MLA_HEADS, 1, MLA_V, tm), lambda b, s: (b, 0, s // per_chunk, 0, s % per_chunk))
    return pl.pallas_call(
        _in_proj_kernel,
        grid=grid,
        in_specs=[tok(D), tok(1)] + [_resident(a.shape) for a in (g1, win, qag, wqb, kvag, wkn, wvt, qg, kg, sqg, skg, rc)],
        out_specs=[heads(MLA_QK_PAD), heads(MLA_QK_PAD), vt_spec, tok(SWA_HEADS * SWA_HD), tok(2 * LANES), tok(2 * LANES)],
        out_shape=[
            jax.ShapeDtypeStruct((B, MLA_HEADS, S, MLA_QK_PAD), BF16),
            jax.ShapeDtypeStruct((B, MLA_HEADS, S, MLA_QK_PAD), BF16),
            jax.ShapeDtypeStruct((B, MLA_HEADS, S // tk, MLA_V, tk), BF16),
            jax.ShapeDtypeStruct((B, S, SWA_HEADS * SWA_HD), BF16),
            jax.ShapeDtypeStruct((B, S, 2 * LANES), BF16),
            jax.ShapeDtypeStruct((B, S, 2 * LANES), BF16),
        ],
        compiler_params=pltpu.CompilerParams(dimension_semantics=("parallel", "parallel"), vmem_limit_bytes=VMEM_LIMIT),
        name="in_proj",
    )(x, pos_col, g1, win, qag, wqb, kvag, wkn, wvt, qg, kg, sqg, skg, rc)


def _mla_attn_kernel(q_ref, k_ref, vt_ref, o_ref, *, tq):
    qi = pl.program_id(2)
    heads = q_ref.shape[1]

    def chunk(j, carry, masked):
        start = pl.multiple_of(j * tq, tq)
        out = []
        scores = [_dot_nt(k_ref[0, hh, pl.ds(start, tq), :], q_ref[0, hh]) for hh in range(heads)]
        for hh in range(heads):
            m, l, acc = carry[hh]
            st = scores[hh]
            if masked:
                key = lax.broadcasted_iota(jnp.int32, (tq, tq), 0)
                qry = lax.broadcasted_iota(jnp.int32, (tq, tq), 1)
                st = jnp.where(key <= qry, st, NEG_INF)
            m_new = jnp.maximum(m, jnp.max(st, axis=0, keepdims=True))
            alpha = jnp.exp2(m - m_new)
            p = jnp.exp2(st - m_new)
            l = alpha * l + jnp.sum(p, axis=0, keepdims=True)
            acc = alpha * acc + _dot(vt_ref[0, hh, j], p.astype(BF16))
            out.append((m_new, l, acc))
        return tuple(out)

    init = tuple((jnp.full((1, tq), NEG_INF, F32), jnp.zeros((1, tq), F32), jnp.zeros((MLA_V, tq), F32))
                 for _ in range(heads))
    carry = lax.fori_loop(0, qi, lambda j, c: chunk(j, c, False), init)
    for hh, (m, l, acc) in enumerate(chunk(qi, carry, True)):
        o_ref[0, :, hh * MLA_V:(hh + 1) * MLA_V] = (acc / l).T


def _mla_attn(qm, km, vm, *, tq, hb):
    B, H, S, _ = qm.shape
    return pl.pallas_call(
        functools.partial(_mla_attn_kernel, tq=tq),
        grid=(B, H // hb, S // tq),
        in_specs=[
            pl.BlockSpec((1, hb, tq, MLA_QK_PAD), lambda b, h, i: (b, h, i, 0)),
            pl.BlockSpec((1, hb, S, MLA_QK_PAD), lambda b, h, i: (b, h, 0, 0)),
            pl.BlockSpec((1, hb, S // tq, MLA_V, tq), lambda b, h, i: (b, h, 0, 0, 0)),
        ],
        out_specs=pl.BlockSpec((1, tq, hb * MLA_V), lambda b, h, i: (b, i, h)),
        out_shape=jax.ShapeDtypeStruct((B, S, H * MLA_V), F32),
        compiler_params=pltpu.CompilerParams(dimension_semantics=("parallel", "parallel", "arbitrary"),
                                             vmem_limit_bytes=VMEM_LIMIT),
        name="mla_attn",
    )(qm, km, vm)


def _t5_bucket(dist):
    n = jnp.maximum(dist, 0)
    max_exact = N_BUCKETS // 2
    nf = jnp.maximum(n, 1).astype(F32)
    large = max_exact + (jnp.log(nf / max_exact) / math.log(MAX_DISTANCE / max_exact)
                         * (N_BUCKETS - max_exact)).astype(jnp.int32)
    large = jnp.minimum(large, N_BUCKETS - 1)
    return jnp.where(n < max_exact, n, large)


def _swa_attn_kernel(sink_ref, q_ref, kp_ref, kc_ref, vp_ref, vc_ref, posq_ref, pkp_ref, pkc_ref, tab_ref, o_ref):
    n = pl.program_id(1)
    kb = jnp.concatenate([kp_ref[0], kc_ref[0]], axis=0)
    vb = jnp.concatenate([vp_ref[0], vc_ref[0]], axis=0)
    kpos = jnp.concatenate([pkp_ref[0], pkc_ref[0]], axis=1)
    bucket = _t5_bucket(posq_ref[0] - kpos)

    row = lax.broadcasted_iota(jnp.int32, (BLOCK, 2 * BLOCK), 0)
    col = lax.broadcasted_iota(jnp.int32, (BLOCK, 2 * BLOCK), 1)
    off = row + BLOCK - col
    valid = (off >= 0) & (off < WINDOW) & ((col >= BLOCK) | (n > 0))

    lane = lax.broadcasted_iota(jnp.int32, (1, LANES), 1)
    lo = lane < SWA_HD
    zero = jnp.zeros((), BF16)

    halves = (lo, jnp.logical_not(lo))
    k_half = [[jnp.where(sel, kb[:, g * LANES:(g + 1) * LANES], zero) for sel in halves] for g in range(SWA_KV_HEADS)]
    v_half = [[jnp.where(sel, vb[:, g * LANES:(g + 1) * LANES], zero) for sel in halves] for g in range(SWA_KV_HEADS)]
    scores = []
    for h in range(SWA_HEADS):
        qp = q_ref[0, :, (h // 2) * LANES:(h // 2 + 1) * LANES]
        scores.append(_dot_nt(qp, k_half[h // SWA_GROUP][h % 2]))

    outs = []
    for pair in range(SWA_HEADS // 2):
        g = (2 * pair) // SWA_GROUP
        o_pair = None
        for half in range(2):
            h = 2 * pair + half
            s = scores[h]
            tab_row = jnp.broadcast_to(tab_ref[h:h + 1, :], (BLOCK, LANES))
            bias = jnp.concatenate(
                [jnp.take_along_axis(tab_row, bucket[:, c * LANES:(c + 1) * LANES], axis=1) for c in range(2)], axis=1)
            s = jnp.where(valid, s + bias, NEG_INF)
            sink = sink_ref[h]
            m = jnp.maximum(jnp.max(s, axis=-1, keepdims=True), sink)
            p = jnp.exp(s - m)
            denom = jnp.sum(p, axis=-1, keepdims=True) + jnp.exp(sink - m)
            p = (p / denom).astype(BF16)
            o = _dot(p, v_half[g][half])
            o_pair = o if o_pair is None else o_pair + o
        outs.append(o_pair)
    o_ref[0] = jnp.concatenate(outs, axis=1)


def _swa_attn(sinks, qs, ks, vs, pos_col, pos_row, tab):
    B, S, _ = qs.shape
    nb = S // BLOCK
    cur = lambda w: pl.BlockSpec((1, BLOCK, w), lambda b, n: (b, n, 0))
    prev = lambda w: pl.BlockSpec((1, BLOCK, w), lambda b, n: (b, jnp.maximum(n - 1, 0), 0))
    return pl.pallas_call(
        _swa_attn_kernel,
        grid=(B, nb),
        in_specs=[
            pl.BlockSpec(memory_space=pltpu.SMEM),
            cur(SWA_HEADS * SWA_HD), prev(2 * LANES), cur(2 * LANES), prev(2 * LANES), cur(2 * LANES),
            cur(1),
            pl.BlockSpec((1, 1, BLOCK), lambda b, n: (b, 0, jnp.maximum(n - 1, 0))),
            pl.BlockSpec((1, 1, BLOCK), lambda b, n: (b, 0, n)),
            _resident(tab.shape),
        ],
        out_specs=cur(SWA_HEADS * SWA_HD),
        out_shape=jax.ShapeDtypeStruct((B, S, SWA_HEADS * SWA_HD), F32),
        compiler_params=pltpu.CompilerParams(dimension_semantics=("parallel", "arbitrary"), vmem_limit_bytes=VMEM_LIMIT),
        name="swa_attn",
    )(sinks, qs, ks, ks, vs, vs, pos_col, pos_row, pos_row, tab)


def _out_proj_kernel(om_ref, os_ref, x_ref, gout_ref, wout_ref, g2_ref, h_ref, n2_ref):
    om = om_ref[...]
    osw = os_ref[...]
    half = om.shape[-1]
    a = om * _rms_scale(om, half) * gout_ref[:, :half]
    b = osw * _rms_scale(osw, half) * gout_ref[:, half:]
    mixed = jnp.concatenate([a.astype(BF16), b.astype(BF16)], axis=1)
    h = x_ref[...] + _dot(mixed, wout_ref[...])
    h_ref[...] = h
    n2_ref[...] = (h * _rms_scale(h, h.shape[-1]) * g2_ref[...]).astype(BF16)


def _out_proj(om, osw, x2, gout, wout, g2, *, tm):
    T, D = x2.shape
    half = om.shape[-1]
    tok = lambda w: pl.BlockSpec((tm, w), lambda t: (t, 0))
    return pl.pallas_call(
        _out_proj_kernel,
        grid=(T // tm,),
        in_specs=[tok(half), tok(half), tok(D), _resident(gout.shape), _resident(wout.shape), _resident(g2.shape)],
        out_specs=[tok(D), tok(D)],
        out_shape=[jax.ShapeDtypeStruct((T, D), F32), jax.ShapeDtypeStruct((T, D), BF16)],
        compiler_params=pltpu.CompilerParams(dimension_semantics=("parallel",), vmem_limit_bytes=VMEM_LIMIT),
        name="out_proj",
    )(om, osw, x2, gout, wout, g2)


_CAND = [(a, b) for a in range(PEER_TOPK) for b in range(PEER_TOPK) if (a + 1) * (b + 1) <= PEER_TOPK]


def _peer_route_kernel(n2_ref, wqt_ref, sk_ref, cnt_ref, a_ref, r2_ref, b_ref,
                       s_scr, rank_scr, vals_scr, n_scr, m_scr, z_scr, *, tr):
    n_lt = tr // LANES
    n_hp = 2 * PEER_HEADS
    qt = lax.dot_general(wqt_ref[...], n2_ref[...], (((1,), (1,)), ((), ())),
                         preferred_element_type=F32).astype(BF16)
    for hp in range(n_hp):
        s_scr[hp] = _dot(sk_ref[hp], qt[hp * PEER_HALF:(hp + 1) * PEER_HALF, :])

    key = lax.broadcasted_iota(jnp.int32, (PEER_NKEYS, LANES), 0).astype(F32)
    head = lax.broadcasted_iota(jnp.int32, (PEER_HEADS, LANES), 0)
    vals_scr[...] = jnp.zeros_like(vals_scr)

    def top16(hp, _):
        h = hp // 2
        p = hp % 2
        for lt in range(n_lt):
            lanes = slice(lt * LANES, (lt + 1) * LANES)
            v = s_scr[hp, :, lanes]
            rank = jnp.full((PEER_NKEYS, LANES), float(PEER_TOPK), F32)
            for r in range(PEER_TOPK):
                m = jnp.max(v, axis=0, keepdims=True)
                first = jnp.min(jnp.where(v == m, key, float(PEER_NKEYS)), axis=0, keepdims=True)
                hit = key == first
                v = jnp.where(hit, -jnp.inf, v)
                rank = jnp.where(hit, float(r), rank)
                vals_scr[p, r, :, lanes] = jnp.where(head == h, m, vals_scr[p, r, :, lanes])
            rank_scr[hp, :, lanes] = rank
        return 0

    lax.fori_loop(0, n_hp, top16, 0)

    for lt in range(n_lt):
        lanes = slice(lt * LANES, (lt + 1) * LANES)
        v1 = [vals_scr[0, a, :, lanes] for a in range(PEER_TOPK)]
        v2 = [vals_scr[1, b, :, lanes] for b in range(PEER_TOPK)]
        c = [v1[a] + v2[b] for (a, b) in _CAND]
        flat = [float(a * PEER_TOPK + b) for (a, b) in _CAND]
        for _ in range(PEER_TOPK):
            m = functools.reduce(jnp.maximum, c)
            first = functools.reduce(jnp.minimum, [jnp.where(ci == m, fi, 1e9) for ci, fi in zip(c, flat)])
            c = [jnp.where(first == fi, -jnp.inf, ci) for ci, fi in zip(c, flat)]
        e1 = [jnp.exp(v1[a] - v1[0]) for a in range(PEER_TOPK)]
        e2 = [jnp.exp(v2[b] - v2[0]) for b in range(PEER_TOPK)]
        z = jnp.zeros_like(v1[0])
        n_a = [jnp.zeros_like(v1[0]) for _ in range(PEER_TOPK)]
        for ci, (a, b) in zip(c, _CAND):
            taken = ci == -jnp.inf
            z = z + jnp.where(taken, e1[a] * e2[b], 0.0)
            n_a[a] = n_a[a] + jnp.where(taken, 1.0, 0.0)
        for a in range(PEER_TOPK):
            n_scr[a, :, lanes] = n_a[a]
        m_scr[0, :, lanes] = v1[0]
        m_scr[1, :, lanes] = v2[0]
        z_scr[:, lanes] = 1.0 / z

    def spread(h, _):
        rank1 = rank_scr[2 * h]
        cnt = jnp.zeros((PEER_NKEYS, tr), F32)
        for a in range(PEER_TOPK):
            cnt = jnp.where(rank1 == float(a), n_scr[a, pl.ds(h, 1), :], cnt)
        a = jnp.exp(s_scr[2 * h] - m_scr[0, pl.ds(h, 1), :]) * z_scr[pl.ds(h, 1), :]
        for lt in range(n_lt):
            cnt_ref[h, lt] = cnt[:, lt * LANES:(lt + 1) * LANES]
            a_ref[h, lt] = a[:, lt * LANES:(lt + 1) * LANES]
        r2_ref[h] = rank_scr[2 * h + 1].astype(BF16)
        b_ref[h] = jnp.exp(s_scr[2 * h + 1] - m_scr[1, pl.ds(h, 1), :]).astype(BF16)
        return 0

    lax.fori_loop(0, PEER_HEADS, spread, 0)


def _peer_route(n2, wqt, sk, *, tr):
    T, D = n2.shape
    route = pl.BlockSpec((PEER_HEADS, PEER_NKEYS, tr), lambda t: (0, 0, t))
    shape = lambda dt: jax.ShapeDtypeStruct((PEER_HEADS, PEER_NKEYS, T), dt)
    slab = pl.BlockSpec((PEER_HEADS, tr // LANES, PEER_NKEYS, LANES), lambda t: (0, t, 0, 0))
    slab_shape = jax.ShapeDtypeStruct((PEER_HEADS, T // LANES, PEER_NKEYS, LANES), F32)
    return pl.pallas_call(
        functools.partial(_peer_route_kernel, tr=tr),
        grid=(T // tr,),
        in_specs=[pl.BlockSpec((tr, D), lambda t: (t, 0)), _resident(wqt.shape), _resident(sk.shape)],
        out_specs=[slab, slab, route, route],
        out_shape=[slab_shape, slab_shape, shape(BF16), shape(BF16)],
        scratch_shapes=[
            pltpu.VMEM((2 * PEER_HEADS, PEER_NKEYS, tr), F32),
            pltpu.VMEM((2 * PEER_HEADS, PEER_NKEYS, tr), F32),
            pltpu.VMEM((2, PEER_TOPK, PEER_HEADS, tr), F32),
            pltpu.VMEM((PEER_TOPK, PEER_HEADS, tr), F32),
            pltpu.VMEM((2, PEER_HEADS, tr), F32),
            pltpu.VMEM((PEER_HEADS, tr), F32),
        ],
        compiler_params=pltpu.CompilerParams(dimension_semantics=("parallel",), vmem_limit_bytes=VMEM_LIMIT),
        name="peer_route",
    )(n2, wqt, sk)


PACK = 16


def _peer_dense_kernel(n2_ref, u_ref, vt_ref, cnt_ref, a_ref, r2_ref, b_ref, h_ref, o_ref, acc_ref, act_scr, w_scr, *, te):
    e = pl.program_id(1)
    tm = n2_ref.shape[0]
    n_sub = te // PEER_NKEYS

    last = pl.num_programs(1) - 1
    cur = e % 2
    zero = jnp.zeros((), BF16)

    def front_matmul():
        act_scr[...] = _dot_nt(u_ref[...], n2_ref[...])

    def back_matmul():
        return _dot(vt_ref[...], w_scr[1 - cur])

    def gate_tile():
        for ii in range(n_sub):
            i = e * n_sub + ii
            row16 = lambda ref, h: jnp.concatenate(
                [ref[h, lt, pl.ds(i, PACK, stride=0), :] for lt in range(tm // LANES)], axis=1).astype(BF16)
            cnt = [row16(cnt_ref, h) for h in range(PEER_HEADS)]
            a = [row16(a_ref, h) for h in range(PEER_HEADS)]
            for jg in range(PEER_NKEYS // PACK):
                keys = slice(jg * PACK, (jg + 1) * PACK)
                gate = None
                for h in range(PEER_HEADS):
                    term = jnp.where(r2_ref[h, keys, :] < cnt[h], b_ref[h, keys, :], zero) * a[h]
                    gate = term if gate is None else gate + term
                rows = slice(ii * PEER_NKEYS + jg * PACK, ii * PEER_NKEYS + (jg + 1) * PACK)
                w_scr[cur, rows, :] = jax.nn.gelu(act_scr[rows, :]).astype(BF16) * gate

    @pl.when(e == 0)
    def _():
        front_matmul()
        acc_ref[...] = jnp.zeros_like(acc_ref)
        gate_tile()

    @pl.when((e > 0) & (e < last))
    def _():
        front_matmul()
        acc_ref[...] += back_matmul()
        gate_tile()

    @pl.when(e == last)
    def _():
        o_ref[...] = h_ref[...] + (acc_ref[...] + back_matmul()).T


def _peer_dense(n2, u, vt, cnt, a, r2, b, h, *, tm, te):
    T, D = n2.shape
    n_tiles = u.shape[0] // te
    route = pl.BlockSpec((PEER_HEADS, PEER_NKEYS, tm), lambda t, e: (0, 0, t))
    route_once = pl.BlockSpec((PEER_HEADS, tm // LANES, PEER_NKEYS, LANES), lambda t, e: (0, t, 0, 0),
                              pipeline_mode=pl.Buffered(1))
    return pl.pallas_call(
        functools.partial(_peer_dense_kernel, te=te),
        grid=(T // tm, n_tiles + 1),
        in_specs=[
            pl.BlockSpec((tm, D), lambda t, e: (t, 0), pipeline_mode=pl.Buffered(1)),
            pl.BlockSpec((te, D), lambda t, e: (jnp.minimum(e, n_tiles - 1), 0)),
            pl.BlockSpec((D, te), lambda t, e: (0, jnp.maximum(e - 1, 0))),
            route_once, route_once, route, route,
            pl.BlockSpec((tm, D), lambda t, e: (t, 0), pipeline_mode=pl.Buffered(1)),
        ],
        out_specs=pl.BlockSpec((tm, D), lambda t, e: (t, 0)),
        out_shape=jax.ShapeDtypeStruct((T, D), F32),
        scratch_shapes=[pltpu.VMEM((D, tm), F32), pltpu.VMEM((te, tm), F32), pltpu.VMEM((2, te, tm), BF16)],
        compiler_params=pltpu.CompilerParams(dimension_semantics=("parallel", "arbitrary"), vmem_limit_bytes=VMEM_LIMIT),
        name="peer_dense",
    )(n2, u, vt, cnt, a, r2, b, h)


def _row(v, width=None):
    v = v.astype(F32).reshape(1, -1)
    if width is not None and v.shape[1] < width:
        v = jnp.pad(v, ((0, 0), (0, width - v.shape[1])))
    return v


def _rope_consts():
    half = MLA_ROPE // 2
    inv_freq = ROPE_THETA ** (-jnp.arange(half, dtype=F32) / half)
    z = jnp.zeros((LANES - MLA_ROPE,), F32)
    rows = [
        jnp.concatenate([inv_freq, inv_freq, z]),
        jnp.concatenate([jnp.ones((MLA_ROPE,), F32), z]),
        jnp.concatenate([-jnp.ones((half,), F32), jnp.ones((half,), F32), z]),
    ]
    return jnp.pad(jnp.stack(rows), ((0, 5), (0, 0)))


def _layer(x, positions, norm1_gain, w_in, q_a_gain, w_q_b, kv_a_gain, w_kv_b, mla_q_gain, mla_k_gain,
           swa_q_gain, swa_k_gain, swa_sinks, rel_bias_table, group_out_gain, w_out, norm2_gain,
           peer_w_q, peer_sub_keys, peer_u, peer_v, *, tm_in, tq, tm_out, tr, tm_peer, te):
    B, S, D = x.shape
    T = B * S

    zpad = jnp.zeros((D, LANES - MLA_ROPE), w_in.dtype)
    o = np.cumsum((MLA_Q_RANK, MLA_KV_RANK, MLA_ROPE, SWA_HEADS * SWA_HD, SWA_KV_HEADS * SWA_HD)).tolist()
    k_swa, v_swa = w_in[:, o[3]:o[4]], w_in[:, o[4]:]
    dup = lambda w: jnp.concatenate([w[:, :SWA_HD], w[:, :SWA_HD], w[:, SWA_HD:], w[:, SWA_HD:]], axis=1)
    win = jnp.concatenate([w_in[:, :o[2]], zpad, w_in[:, o[2]:o[3]], dup(k_swa), dup(v_swa)], axis=1).astype(BF16)
    wqb = jnp.pad(w_q_b.reshape(MLA_Q_RANK, MLA_HEADS, MLA_QK), ((0, 0), (0, 0), (0, MLA_QK_PAD - MLA_QK)))
    wqb = wqb.reshape(MLA_Q_RANK, MLA_HEADS * MLA_QK_PAD).astype(BF16)

    wkv = w_kv_b.reshape(MLA_KV_RANK, MLA_HEADS, MLA_NOPE + MLA_V)
    wkn = wkv[:, :, :MLA_NOPE].reshape(MLA_KV_RANK, MLA_HEADS * MLA_NOPE).astype(BF16)
    wvt = wkv[:, :, MLA_NOPE:].transpose(1, 2, 0).astype(BF16)

    qm, km, vm, qs, ks, vs = _in_proj(
        x, positions.reshape(B, S, 1), _row(norm1_gain), win, _row(q_a_gain), wqb, _row(kv_a_gain),
        wkn, wvt, _row(mla_q_gain, MLA_QK_PAD), _row(mla_k_gain, MLA_QK_PAD),
        _row(jnp.tile(swa_q_gain, 2)), _row(jnp.tile(swa_k_gain, 2)), _rope_consts(), tm=tm_in, tk=tq)

    o_mla = _mla_attn(qm, km, vm, tq=tq, hb=4)
    tab = jnp.pad(rel_bias_table.astype(F32).T, ((0, 0), (0, LANES - N_BUCKETS)))
    o_swa = _swa_attn(swa_sinks.astype(F32), qs, ks, vs, positions.reshape(B, S, 1), positions.reshape(B, 1, S), tab)

    h, n2 = _out_proj(o_mla.reshape(T, -1), o_swa.reshape(T, -1), x.reshape(T, D), _row(group_out_gain),
                      w_out.astype(BF16), _row(norm2_gain), tm=tm_out)

    sk = peer_sub_keys.reshape(2 * PEER_HEADS, PEER_NKEYS, PEER_HALF).astype(BF16)
    cnt, a, r2, b = _peer_route(n2, peer_w_q.T.astype(BF16), sk, tr=tr)
    out = _peer_dense(n2, peer_u.astype(BF16), peer_v.T.astype(BF16), cnt, a, r2, b, h, tm=tm_peer, te=te)
    return out.reshape(B, S, D)


def kernel(x, positions, norm1_gain, w_in, q_a_gain, w_q_b, kv_a_gain, w_kv_b, mla_q_gain, mla_k_gain, swa_q_gain, swa_k_gain, swa_sinks, rel_bias_table, group_out_gain, w_out, norm2_gain, peer_w_q, peer_sub_keys, peer_u, peer_v):
    assert norm1_gain.shape[0] == 1, "single-layer trunk"
    return _layer(x, positions, norm1_gain[0], w_in[0], q_a_gain[0], w_q_b[0], kv_a_gain[0], w_kv_b[0],
                  mla_q_gain[0], mla_k_gain[0], swa_q_gain[0], swa_k_gain[0], swa_sinks[0], rel_bias_table,
                  group_out_gain[0], w_out[0], norm2_gain[0], peer_w_q[0], peer_sub_keys[0], peer_u[0], peer_v[0],
                  tm_in=256, tq=512, tm_out=512, tr=256, tm_peer=512, te=1024)
```

```python
import functools
import math

import jax
import jax.numpy as jnp
import numpy as np
from jax import lax
from jax.experimental import pallas as pl
from jax.experimental.pallas import tpu as pltpu

EPS = 1e-6
NEG_INF = -1e30
LANES = 128
VMEM_LIMIT = 56 << 20

MLA_HEADS = 8
MLA_NOPE = 128
MLA_ROPE = 64
MLA_V = 128
MLA_QK = MLA_NOPE + MLA_ROPE
MLA_QK_PAD = 256
MLA_Q_RANK = 512
MLA_KV_RANK = 256
ROPE_THETA = 10000.0

SWA_HEADS = 16
SWA_KV_HEADS = 2
SWA_HD = 64
SWA_GROUP = SWA_HEADS // SWA_KV_HEADS
WINDOW = 128
BLOCK = 128
N_BUCKETS = 32
MAX_DISTANCE = 128

PEER_HEADS = 8
PEER_NKEYS = 128
PEER_TOPK = 16
PEER_HALF = 128

BF16 = jnp.bfloat16
F32 = jnp.float32


def _resident(shape):
    nd = len(shape)
    return pl.BlockSpec(shape, lambda *_: (0,) * nd, pipeline_mode=pl.Buffered(1))


def _rms_scale(x, width):
    return lax.rsqrt(jnp.sum(x * x, axis=-1, keepdims=True) * (1.0 / width) + EPS)


def _dot(a, b):
    return jnp.dot(a, b, preferred_element_type=F32)


def _dot_nt(a, b):
    return lax.dot_general(a, b, (((1,), (1,)), ((), ())), preferred_element_type=F32)


_C_QLAT = 0
_C_KVLAT = _C_QLAT + MLA_Q_RANK
_C_KPE = _C_KVLAT + MLA_KV_RANK
_C_QSWA = _C_KPE + LANES
_C_KSWA = _C_QSWA + SWA_HEADS * SWA_HD
_C_VSWA = _C_KSWA + 2 * LANES
_C_END = _C_VSWA + 2 * LANES


def _rope(x, cos_t, sin_t):
    partner = pltpu.roll(x, 32, axis=1) + pltpu.roll(x, 96, axis=1)
    return x * cos_t + partner * sin_t


def _in_proj_kernel(x_ref, pos_ref, g1_ref, win_ref, qag_ref, wqb_ref, kvag_ref, wkn_ref, wvt_ref,
                    qg_ref, kg_ref, sqg_ref, skg_ref, rc_ref,
                    qm_ref, km_ref, vm_ref, qs_ref, ks_ref, vs_ref):
    x = x_ref[0]
    n1 = x * _rms_scale(x, x.shape[-1]) * g1_ref[...]
    proj = _dot(n1.astype(BF16), win_ref[...])

    pos = pos_ref[0].astype(F32)
    ang = pos * rc_ref[0:1, :]
    cos_t = jnp.cos(ang) * rc_ref[1:2, :]
    sin_t = jnp.sin(ang) * rc_ref[2:3, :]

    q_lat = proj[:, _C_QLAT:_C_QLAT + MLA_Q_RANK]
    ql = q_lat * _rms_scale(q_lat, MLA_Q_RANK) * qag_ref[...]
    q = _dot(ql.astype(BF16), wqb_ref[...])
    q_scale = MLA_QK ** -0.5 * math.log2(math.e)
    for h in range(MLA_HEADS):
        qh = q[:, h * MLA_QK_PAD:(h + 1) * MLA_QK_PAD]
        qn = qh * _rms_scale(qh, MLA_QK) * qg_ref[...]
        qr = _rope(qn[:, LANES:], cos_t, sin_t)
        qm_ref[0, h, :, 0:LANES] = (qn[:, :LANES] * q_scale).astype(BF16)
        qm_ref[0, h, :, LANES:] = (qr * q_scale).astype(BF16)

    kv_lat = proj[:, _C_KVLAT:_C_KVLAT + MLA_KV_RANK]
    kvl = kv_lat * _rms_scale(kv_lat, MLA_KV_RANK) * kvag_ref[...]
    kvl = kvl.astype(BF16)
    k_nope = _dot(kvl, wkn_ref[...])
    kpe = proj[:, _C_KPE:_C_KPE + LANES]
    kpe_ss = jnp.sum(kpe * kpe, axis=-1, keepdims=True)
    kr = _rope(kpe * kg_ref[:, LANES:], cos_t, sin_t)
    for h in range(MLA_HEADS):
        kn = k_nope[:, h * MLA_NOPE:(h + 1) * MLA_NOPE]
        ss = jnp.sum(kn * kn, axis=-1, keepdims=True) + kpe_ss
        r = lax.rsqrt(ss * (1.0 / MLA_QK) + EPS)
        km_ref[0, h, :, 0:LANES] = (kn * r * kg_ref[:, :LANES]).astype(BF16)
        km_ref[0, h, :, LANES:] = (kr * r).astype(BF16)
        vm_ref[0, h, 0] = _dot_nt(wvt_ref[h], kvl).astype(BF16)

    lane = lax.broadcasted_iota(jnp.int32, (1, LANES), 1)
    lo = lane < SWA_HD
    s_scale = SWA_HD ** -0.5
    for p in range(SWA_HEADS // 2):
        v = proj[:, _C_QSWA + p * LANES:_C_QSWA + (p + 1) * LANES]
        sq = v * v
        ss_lo = jnp.sum(jnp.where(lo, sq, 0.0), axis=-1, keepdims=True)
        ss_hi = jnp.sum(jnp.where(lo, 0.0, sq), axis=-1, keepdims=True)
        r = jnp.where(lo, lax.rsqrt(ss_lo * (1.0 / SWA_HD) + EPS), lax.rsqrt(ss_hi * (1.0 / SWA_HD) + EPS))
        qs_ref[0, :, p * LANES:(p + 1) * LANES] = (v * r * sqg_ref[...] * s_scale).astype(BF16)
    for g in range(SWA_KV_HEADS):
        v = proj[:, _C_KSWA + g * LANES:_C_KSWA + (g + 1) * LANES]
        r = lax.rsqrt(jnp.sum(v * v, axis=-1, keepdims=True) * (0.5 / SWA_HD) + EPS)
        ks_ref[0, :, g * LANES:(g + 1) * LANES] = (v * r * skg_ref[...]).astype(BF16)
    vs_ref[0] = proj[:, _C_VSWA:_C_END].astype(BF16)


def _in_proj(x, pos_col, g1, win, qag, wqb, kvag, wkn, wvt, qg, kg, sqg, skg, rc, *, tm, tk):
    B, S, D = x.shape
    grid = (B, S // tm)
    per_chunk = tk // tm
    tok = lambda w: pl.BlockSpec((1, tm, w), lambda b, s: (b, s, 0))
    heads = lambda w: pl.BlockSpec((1, MLA_HEADS, tm, w), lambda b, s: (b, 0, s, 0))
    vt_spec = pl.BlockSpec((1, MLA_HEADS, 1, MLA_V, tm), lambda b, s: (b, 0, s // per_chunk, 0, s % per_chunk))
    return pl.pallas_call(
        _in_proj_kernel,
        grid=grid,
        in_specs=[tok(D), tok(1)] + [_resident(a.shape) for a in (g1, win, qag, wqb, kvag, wkn, wvt, qg, kg, sqg, skg, rc)],
        out_specs=[heads(MLA_QK_PAD), heads(MLA_QK_PAD), vt_spec, tok(SWA_HEADS * SWA_HD), tok(2 * LANES), tok(2 * LANES)],
        out_shape=[
            jax.ShapeDtypeStruct((B, MLA_HEADS, S, MLA_QK_PAD), BF16),
            jax.ShapeDtypeStruct((B, MLA_HEADS, S, MLA_QK_PAD), BF16),
            jax.ShapeDtypeStruct((B, MLA_HEADS, S // tk, MLA_V, tk), BF16),
            jax.ShapeDtypeStruct((B, S, SWA_HEADS * SWA_HD), BF16),
            jax.ShapeDtypeStruct((B, S, 2 * LANES), BF16),
            jax.ShapeDtypeStruct((B, S, 2 * LANES), BF16),
        ],
        compiler_params=pltpu.CompilerParams(dimension_semantics=("parallel", "parallel"), vmem_limit_bytes=VMEM_LIMIT),
        name="in_proj",
    )(x, pos_col, g1, win, qag, wqb, kvag, wkn, wvt, qg, kg, sqg, skg, rc)


def _mla_attn_kernel(q_ref, k_ref, vt_ref, o_ref, *, tq):
    qi = pl.program_id(2)
    heads = q_ref.shape[1]

    def chunk(j, carry, masked):
        start = pl.multiple_of(j * tq, tq)
        out = []
        scores = [_dot_nt(k_ref[0, hh, pl.ds(start, tq), :], q_ref[0, hh]) for hh in range(heads)]
        for hh in range(heads):
            m, l, acc = carry[hh]
            st = scores[hh]
            if masked:
                key = lax.broadcasted_iota(jnp.int32, (tq, tq), 0)
                qry = lax.broadcasted_iota(jnp.int32, (tq, tq), 1)
                st = jnp.where(key <= qry, st, NEG_INF)
            m_new = jnp.maximum(m, jnp.max(st, axis=0, keepdims=True))
            alpha = jnp.exp2(m - m_new)
            p = jnp.exp2(st - m_new)
            l = alpha * l + jnp.sum(p, axis=0, keepdims=True)
            acc = alpha * acc + _dot(vt_ref[0, hh, j], p.astype(BF16))
            out.append((m_new, l, acc))
        return tuple(out)

    init = tuple((jnp.full((1, tq), NEG_INF, F32), jnp.zeros((1, tq), F32), jnp.zeros((MLA_V, tq), F32))
                 for _ in range(heads))
    carry = lax.fori_loop(0, qi, lambda j, c: chunk(j, c, False), init)
    for hh, (m, l, acc) in enumerate(chunk(qi, carry, True)):
        o_ref[0, :, hh * MLA_V:(hh + 1) * MLA_V] = (acc / l).T


def _mla_attn(qm, km, vm, *, tq, hb):
    B, H, S, _ = qm.shape
    return pl.pallas_call(
        functools.partial(_mla_attn_kernel, tq=tq),
        grid=(B, H // hb, S // tq),
        in_specs=[
            pl.BlockSpec((1, hb, tq, MLA_QK_PAD), lambda b, h, i: (b, h, i, 0)),
            pl.BlockSpec((1, hb, S, MLA_QK_PAD), lambda b, h, i: (b, h, 0, 0)),
            pl.BlockSpec((1, hb, S // tq, MLA_V, tq), lambda b, h, i: (b, h, 0, 0, 0)),
        ],
        out_specs=pl.BlockSpec((1, tq, hb * MLA_V), lambda b, h, i: (b, i, h)),
        out_shape=jax.ShapeDtypeStruct((B, S, H * MLA_V), F32),
        compiler_params=pltpu.CompilerParams(dimension_semantics=("parallel", "parallel", "arbitrary"),
                                             vmem_limit_bytes=VMEM_LIMIT),
        name="mla_attn",
    )(qm, km, vm)


def _t5_bucket(dist):
    n = jnp.maximum(dist, 0)
    max_exact = N_BUCKETS // 2
    nf = jnp.maximum(n, 1).astype(F32)
    large = max_exact + (jnp.log(nf / max_exact) / math.log(MAX_DISTANCE / max_exact)
                         * (N_BUCKETS - max_exact)).astype(jnp.int32)
    large = jnp.minimum(large, N_BUCKETS - 1)
    return jnp.where(n < max_exact, n, large)


def _swa_attn_kernel(sink_ref, q_ref, kp_ref, kc_ref, vp_ref, vc_ref, posq_ref, pkp_ref, pkc_ref, tab_ref, o_ref):
    n = pl.program_id(1)
    kb = jnp.concatenate([kp_ref[0], kc_ref[0]], axis=0)
    vb = jnp.concatenate([vp_ref[0], vc_ref[0]], axis=0)
    kpos = jnp.concatenate([pkp_ref[0], pkc_ref[0]], axis=1)
    bucket = _t5_bucket(posq_ref[0] - kpos)

    row = lax.broadcasted_iota(jnp.int32, (BLOCK, 2 * BLOCK), 0)
    col = lax.broadcasted_iota(jnp.int32, (BLOCK, 2 * BLOCK), 1)
    off = row + BLOCK - col
    valid = (off >= 0) & (off < WINDOW) & ((col >= BLOCK) | (n > 0))

    lane = lax.broadcasted_iota(jnp.int32, (1, LANES), 1)
    lo = lane < SWA_HD
    zero = jnp.zeros((), BF16)

    halves = (lo, jnp.logical_not(lo))
    k_half = [[jnp.where(sel, kb[:, g * LANES:(g + 1) * LANES], zero) for sel in halves] for g in range(SWA_KV_HEADS)]
    v_half = [[jnp.where(sel, vb[:, g * LANES:(g + 1) * LANES], zero) for sel in halves] for g in range(SWA_KV_HEADS)]
    scores = []
    for h in range(SWA_HEADS):
        qp = q_ref[0, :, (h // 2) * LANES:(h // 2 + 1) * LANES]
        scores.append(_dot_nt(qp, k_half[h // SWA_GROUP][h % 2]))

    outs = []
    for pair in range(SWA_HEADS // 2):
        g = (2 * pair) // SWA_GROUP
        o_pair = None
        for half in range(2):
            h = 2 * pair + half
            s = scores[h]
            tab_row = jnp.broadcast_to(tab_ref[h:h + 1, :], (BLOCK, LANES))
            bias = jnp.concatenate(
                [jnp.take_along_axis(tab_row, bucket[:, c * LANES:(c + 1) * LANES], axis=1) for c in range(2)], axis=1)
            s = jnp.where(valid, s + bias, NEG_INF)
            sink = sink_ref[h]
            m = jnp.maximum(jnp.max(s, axis=-1, keepdims=True), sink)
            p = jnp.exp(s - m)
            denom = jnp.sum(p, axis=-1, keepdims=True) + jnp.exp(sink - m)
            p = (p / denom).astype(BF16)
            o = _dot(p, v_half[g][half])
            o_pair = o if o_pair is None else o_pair + o
        outs.append(o_pair)
    o_ref[0] = jnp.concatenate(outs, axis=1)


def _swa_attn(sinks, qs, ks, vs, pos_col, pos_row, tab):
    B, S, _ = qs.shape
    nb = S // BLOCK
    cur = lambda w: pl.BlockSpec((1, BLOCK, w), lambda b, n: (b, n, 0))
    prev = lambda w: pl.BlockSpec((1, BLOCK, w), lambda b, n: (b, jnp.maximum(n - 1, 0), 0))
    return pl.pallas_call(
        _swa_attn_kernel,
        grid=(B, nb),
        in_specs=[
            pl.BlockSpec(memory_space=pltpu.SMEM),
            cur(SWA_HEADS * SWA_HD), prev(2 * LANES), cur(2 * LANES), prev(2 * LANES), cur(2 * LANES),
            cur(1),
            pl.BlockSpec((1, 1, BLOCK), lambda b, n: (b, 0, jnp.maximum(n - 1, 0))),
            pl.BlockSpec((1, 1, BLOCK), lambda b, n: (b, 0, n)),
            _resident(tab.shape),
        ],
        out_specs=cur(SWA_HEADS * SWA_HD),
        out_shape=jax.ShapeDtypeStruct((B, S, SWA_HEADS * SWA_HD), F32),
        compiler_params=pltpu.CompilerParams(dimension_semantics=("parallel", "arbitrary"), vmem_limit_bytes=VMEM_LIMIT),
        name="swa_attn",
    )(sinks, qs, ks, ks, vs, vs, pos_col, pos_row, pos_row, tab)


def _out_proj_kernel(om_ref, os_ref, x_ref, gout_ref, wout_ref, g2_ref, h_ref, n2_ref):
    om = om_ref[...]
    osw = os_ref[...]
    half = om.shape[-1]
    a = om * _rms_scale(om, half) * gout_ref[:, :half]
    b = osw * _rms_scale(osw, half) * gout_ref[:, half:]
    mixed = jnp.concatenate([a.astype(BF16), b.astype(BF16)], axis=1)
    h = x_ref[...] + _dot(mixed, wout_ref[...])
    h_ref[...] = h
    n2_ref[...] = (h * _rms_scale(h, h.shape[-1]) * g2_ref[...]).astype(BF16)


def _out_proj(om, osw, x2, gout, wout, g2, *, tm):
    T, D = x2.shape
    half = om.shape[-1]
    tok = lambda w: pl.BlockSpec((tm, w), lambda t: (t, 0))
    return pl.pallas_call(
        _out_proj_kernel,
        grid=(T // tm,),
        in_specs=[tok(half), tok(half), tok(D), _resident(gout.shape), _resident(wout.shape), _resident(g2.shape)],
        out_specs=[tok(D), tok(D)],
        out_shape=[jax.ShapeDtypeStruct((T, D), F32), jax.ShapeDtypeStruct((T, D), BF16)],
        compiler_params=pltpu.CompilerParams(dimension_semantics=("parallel",), vmem_limit_bytes=VMEM_LIMIT),
        name="out_proj",
    )(om, osw, x2, gout, wout, g2)


_CAND = [(a, b) for a in range(PEER_TOPK) for b in range(PEER_TOPK) if (a + 1) * (b + 1) <= PEER_TOPK]


def _oddeven_merge(lo, hi, r):
    step = r * 2
    if step < hi - lo:
        yield from _oddeven_merge(lo, hi, step)
        yield from _oddeven_merge(lo + r, hi, step)
        yield from [(i, i + r) for i in range(lo + r, hi - r, step)]
    else:
        yield (lo, lo + r)


def _oddeven_merge_sort(lo, hi):
    if hi - lo >= 1:
        mid = lo + (hi - lo) // 2
        yield from _oddeven_merge_sort(lo, mid)
        yield from _oddeven_merge_sort(mid + 1, hi)
        yield from _oddeven_merge(lo, hi, 1)


_SORT16 = tuple(_oddeven_merge_sort(0, PEER_TOPK - 1))
_BITONIC16 = tuple((i, i + s) for s in (8, 4, 2, 1) for i in range(PEER_TOPK) if not i & s)


def _peer_route_kernel(n2_ref, wqt_ref, sk_ref, cnt_ref, a_ref, r2_ref, b_ref,
                       s_scr, rank_scr, vals_scr, n_scr, m_scr, z_scr, *, tr):
    n_lt = tr // LANES
    n_hp = 2 * PEER_HEADS
    qt = lax.dot_general(wqt_ref[...], n2_ref[...], (((1,), (1,)), ((), ())),
                         preferred_element_type=F32).astype(BF16)
    for hp in range(n_hp):
        s_scr[hp] = _dot(sk_ref[hp], qt[hp * PEER_HALF:(hp + 1) * PEER_HALF, :])

    key = lax.broadcasted_iota(jnp.int32, (PEER_NKEYS, LANES), 0).astype(F32)
    head = lax.broadcasted_iota(jnp.int32, (PEER_HEADS, LANES), 0)
    vals_scr[...] = jnp.zeros_like(vals_scr)

    sub = 8
    n_vr = PEER_NKEYS // sub

    def exchange(vals, i, j):
        vals[i], vals[j] = jnp.maximum(vals[i], vals[j]), jnp.minimum(vals[i], vals[j])

    def top16_network(hp, tie):
        h = hp // 2
        p = hp % 2
        for lt in range(n_lt):
            lanes = slice(lt * LANES, (lt + 1) * LANES)
            x = [s_scr[hp, vi * sub:(vi + 1) * sub, lanes] for vi in range(n_vr)]
            top = list(x)
            for i, j in _SORT16:
                exchange(top, i, j)
            for shift in (4, 2, 1):
                other = [pltpu.roll(top[PEER_TOPK - 1 - r], shift, axis=0) for r in range(PEER_TOPK)]
                top = [jnp.maximum(top[r], other[r]) for r in range(PEER_TOPK)]
                for i, j in _BITONIC16:
                    exchange(top, i, j)
            for r in range(PEER_TOPK):
                vals_scr[p, r, :, lanes] = jnp.where(head == h, top[r], vals_scr[p, r, :, lanes])
                if r:
                    tie = jnp.where(top[r - 1] == top[r], 1.0, tie)
            inside = None
            for vi in range(n_vr):
                rank = jnp.zeros((sub, LANES), F32)
                for r in range(PEER_TOPK):
                    rank = jnp.where(top[r] > x[vi], float(r + 1), rank)
                rank_scr[hp, vi * sub:(vi + 1) * sub, lanes] = rank
                ins = jnp.where(x[vi] >= top[PEER_TOPK - 1], 1.0, 0.0)
                inside = ins if inside is None else inside + ins
            for shift in (4, 2, 1):
                inside = inside + pltpu.roll(inside, shift, axis=0)
            tie = jnp.where(inside != float(PEER_TOPK), 1.0, tie)
        return tie

    tie = lax.fori_loop(0, n_hp, top16_network, jnp.zeros((sub, LANES), F32))

    def top16(hp, _):
        h = hp // 2
        p = hp % 2
        for lt in range(n_lt):
            lanes = slice(lt * LANES, (lt + 1) * LANES)
            v = s_scr[hp, :, lanes]
            rank = jnp.full((PEER_NKEYS, LANES), float(PEER_TOPK), F32)
            for r in range(PEER_TOPK):
                m = jnp.max(v, axis=0, keepdims=True)
                first = jnp.min(jnp.where(v == m, key, float(PEER_NKEYS)), axis=0, keepdims=True)
                hit = key == first
                v = jnp.where(hit, -jnp.inf, v)
                rank = jnp.where(hit, float(r), rank)
                vals_scr[p, r, :, lanes] = jnp.where(head == h, m, vals_scr[p, r, :, lanes])
            rank_scr[hp, :, lanes] = rank
        return 0

    @pl.when(jnp.max(tie) > 0.0)
    def _():
        lax.fori_loop(0, n_hp, top16, 0)

    for lt in range(n_lt):
        lanes = slice(lt * LANES, (lt + 1) * LANES)
        v1 = [vals_scr[0, a, :, lanes] for a in range(PEER_TOPK)]
        v2 = [vals_scr[1, b, :, lanes] for b in range(PEER_TOPK)]
        c = [v1[a] + v2[b] for (a, b) in _CAND]
        flat = [float(a * PEER_TOPK + b) for (a, b) in _CAND]
        for _ in range(PEER_TOPK):
            m = functools.reduce(jnp.maximum, c)
            first = functools.reduce(jnp.minimum, [jnp.where(ci == m, fi, 1e9) for ci, fi in zip(c, flat)])
            c = [jnp.where(first == fi, -jnp.inf, ci) for ci, fi in zip(c, flat)]
        e1 = [jnp.exp(v1[a] - v1[0]) for a in range(PEER_TOPK)]
        e2 = [jnp.exp(v2[b] - v2[0]) for b in range(PEER_TOPK)]
        z = jnp.zeros_like(v1[0])
        n_a = [jnp.zeros_like(v1[0]) for _ in range(PEER_TOPK)]
        for ci, (a, b) in zip(c, _CAND):
            taken = ci == -jnp.inf
            z = z + jnp.where(taken, e1[a] * e2[b], 0.0)
            n_a[a] = n_a[a] + jnp.where(taken, 1.0, 0.0)
        for a in range(PEER_TOPK):
            n_scr[a, :, lanes] = n_a[a]
        m_scr[0, :, lanes] = v1[0]
        m_scr[1, :, lanes] = v2[0]
        z_scr[:, lanes] = 1.0 / z

    def spread(h, _):
        rank1 = rank_scr[2 * h]
        cnt = jnp.zeros((PEER_NKEYS, tr), F32)
        for a in range(PEER_TOPK):
            cnt = jnp.where(rank1 == float(a), n_scr[a, pl.ds(h, 1), :], cnt)
        a = jnp.exp(s_scr[2 * h] - m_scr[0, pl.ds(h, 1), :]) * z_scr[pl.ds(h, 1), :]
        for lt in range(n_lt):
            cnt_ref[h, lt] = cnt[:, lt * LANES:(lt + 1) * LANES]
            a_ref[h, lt] = a[:, lt * LANES:(lt + 1) * LANES]
        r2_ref[h] = rank_scr[2 * h + 1].astype(BF16)
        b_ref[h] = jnp.exp(s_scr[2 * h + 1] - m_scr[1, pl.ds(h, 1), :]).astype(BF16)
        return 0

    lax.fori_loop(0, PEER_HEADS, spread, 0)


def _peer_route(n2, wqt, sk, *, tr):
    T, D = n2.shape
    route = pl.BlockSpec((PEER_HEADS, PEER_NKEYS, tr), lambda t: (0, 0, t))
    shape = lambda dt: jax.ShapeDtypeStruct((PEER_HEADS, PEER_NKEYS, T), dt)
    slab = pl.BlockSpec((PEER_HEADS, tr // LANES, PEER_NKEYS, LANES), lambda t: (0, t, 0, 0))
    slab_shape = jax.ShapeDtypeStruct((PEER_HEADS, T // LANES, PEER_NKEYS, LANES), F32)
    return pl.pallas_call(
        functools.partial(_peer_route_kernel, tr=tr),
        grid=(T // tr,),
        in_specs=[pl.BlockSpec((tr, D), lambda t: (t, 0)), _resident(wqt.shape), _resident(sk.shape)],
        out_specs=[slab, slab, route, route],
        out_shape=[slab_shape, slab_shape, shape(BF16), shape(BF16)],
        scratch_shapes=[
            pltpu.VMEM((2 * PEER_HEADS, PEER_NKEYS, tr), F32),
            pltpu.VMEM((2 * PEER_HEADS, PEER_NKEYS, tr), F32),
            pltpu.VMEM((2, PEER_TOPK, PEER_HEADS, tr), F32),
            pltpu.VMEM((PEER_TOPK, PEER_HEADS, tr), F32),
            pltpu.VMEM((2, PEER_HEADS, tr), F32),
            pltpu.VMEM((PEER_HEADS, tr), F32),
        ],
        compiler_params=pltpu.CompilerParams(dimension_semantics=("parallel",), vmem_limit_bytes=VMEM_LIMIT),
        name="peer_route",
    )(n2, wqt, sk)


PACK = 16


def _peer_dense_kernel(n2_ref, u_ref, vt_ref, cnt_ref, a_ref, r2_ref, b_ref, h_ref, o_ref, acc_ref, act_scr, w_scr, *, te):
    e = pl.program_id(1)
    tm = n2_ref.shape[0]
    n_sub = te // PEER_NKEYS

    last = pl.num_programs(1) - 1
    cur = e % 2
    zero = jnp.zeros((), BF16)

    def front_matmul():
        act_scr[...] = _dot_nt(u_ref[...], n2_ref[...])

    def back_matmul():
        return _dot(vt_ref[...], w_scr[1 - cur])

    def gate_tile():
        for ii in range(n_sub):
            i = e * n_sub + ii
            row16 = lambda ref, h: jnp.concatenate(
                [ref[h, lt, pl.ds(i, PACK, stride=0), :] for lt in range(tm // LANES)], axis=1).astype(BF16)
            cnt = [row16(cnt_ref, h) for h in range(PEER_HEADS)]
            a = [row16(a_ref, h) for h in range(PEER_HEADS)]
            for jg in range(PEER_NKEYS // PACK):
                keys = slice(jg * PACK, (jg + 1) * PACK)
                gate = None
                for h in range(PEER_HEADS):
                    term = jnp.where(r2_ref[h, keys, :] < cnt[h], b_ref[h, keys, :], zero) * a[h]
                    gate = term if gate is None else gate + term
                rows = slice(ii * PEER_NKEYS + jg * PACK, ii * PEER_NKEYS + (jg + 1) * PACK)
                w_scr[cur, rows, :] = jax.nn.gelu(act_scr[rows, :]).astype(BF16) * gate

    @pl.when(e == 0)
    def _():
        front_matmul()
        acc_ref[...] = jnp.zeros_like(acc_ref)
        gate_tile()

    @pl.when((e > 0) & (e < last))
    def _():
        front_matmul()
        acc_ref[...] += back_matmul()
        gate_tile()

    @pl.when(e == last)
    def _():
        o_ref[...] = h_ref[...] + (acc_ref[...] + back_matmul()).T


def _peer_dense(n2, u, vt, cnt, a, r2, b, h, *, tm, te):
    T, D = n2.shape
    n_tiles = u.shape[0] // te
    route = pl.BlockSpec((PEER_HEADS, PEER_NKEYS, tm), lambda t, e: (0, 0, t))
    route_once = pl.BlockSpec((PEER_HEADS, tm // LANES, PEER_NKEYS, LANES), lambda t, e: (0, t, 0, 0),
                              pipeline_mode=pl.Buffered(1))
    return pl.pallas_call(
        functools.partial(_peer_dense_kernel, te=te),
        grid=(T // tm, n_tiles + 1),
        in_specs=[
            pl.BlockSpec((tm, D), lambda t, e: (t, 0), pipeline_mode=pl.Buffered(1)),
            pl.BlockSpec((te, D), lambda t, e: (jnp.minimum(e, n_tiles - 1), 0)),
            pl.BlockSpec((D, te), lambda t, e: (0, jnp.maximum(e - 1, 0))),
            route_once, route_once, route, route,
            pl.BlockSpec((tm, D), lambda t, e: (t, 0), pipeline_mode=pl.Buffered(1)),
        ],
        out_specs=pl.BlockSpec((tm, D), lambda t, e: (t, 0)),
        out_shape=jax.ShapeDtypeStruct((T, D), F32),
        scratch_shapes=[pltpu.VMEM((D, tm), F32), pltpu.VMEM((te, tm), F32), pltpu.VMEM((2, te, tm), BF16)],
        compiler_params=pltpu.CompilerParams(dimension_semantics=("parallel", "arbitrary"), vmem_limit_bytes=VMEM_LIMIT),
        name="peer_dense",
    )(n2, u, vt, cnt, a, r2, b, h)


def _row(v, width=None):
    v = v.astype(F32).reshape(1, -1)
    if width is not None and v.shape[1] < width:
        v = jnp.pad(v, ((0, 0), (0, width - v.shape[1])))
    return v


def _rope_consts():
    half = MLA_ROPE // 2
    inv_freq = ROPE_THETA ** (-jnp.arange(half, dtype=F32) / half)
    z = jnp.zeros((LANES - MLA_ROPE,), F32)
    rows = [
        jnp.concatenate([inv_freq, inv_freq, z]),
        jnp.concatenate([jnp.ones((MLA_ROPE,), F32), z]),
        jnp.concatenate([-jnp.ones((half,), F32), jnp.ones((half,), F32), z]),
    ]
    return jnp.pad(jnp.stack(rows), ((0, 5), (0, 0)))


def _layer(x, positions, norm1_gain, w_in, q_a_gain, w_q_b, kv_a_gain, w_kv_b, mla_q_gain, mla_k_gain,
           swa_q_gain, swa_k_gain, swa_sinks, rel_bias_table, group_out_gain, w_out, norm2_gain,
           peer_w_q, peer_sub_keys, peer_u, peer_v, *, tm_in, tq, tm_out, tr, tm_peer, te):
    B, S, D = x.shape
    T = B * S

    zpad = jnp.zeros((D, LANES - MLA_ROPE), w_in.dtype)
    o = np.cumsum((MLA_Q_RANK, MLA_KV_RANK, MLA_ROPE, SWA_HEADS * SWA_HD, SWA_KV_HEADS * SWA_HD)).tolist()
    k_swa, v_swa = w_in[:, o[3]:o[4]], w_in[:, o[4]:]
    dup = lambda w: jnp.concatenate([w[:, :SWA_HD], w[:, :SWA_HD], w[:, SWA_HD:], w[:, SWA_HD:]], axis=1)
    win = jnp.concatenate([w_in[:, :o[2]], zpad, w_in[:, o[2]:o[3]], dup(k_swa), dup(v_swa)], axis=1).astype(BF16)
    wqb = jnp.pad(w_q_b.reshape(MLA_Q_RANK, MLA_HEADS, MLA_QK), ((0, 0), (0, 0), (0, MLA_QK_PAD - MLA_QK)))
    wqb = wqb.reshape(MLA_Q_RANK, MLA_HEADS * MLA_QK_PAD).astype(BF16)

    wkv = w_kv_b.reshape(MLA_KV_RANK, MLA_HEADS, MLA_NOPE + MLA_V)
    wkn = wkv[:, :, :MLA_NOPE].reshape(MLA_KV_RANK, MLA_HEADS * MLA_NOPE).astype(BF16)
    wvt = wkv[:, :, MLA_NOPE:].transpose(1, 2, 0).astype(BF16)

    qm, km, vm, qs, ks, vs = _in_proj(
        x, positions.reshape(B, S, 1), _row(norm1_gain), win, _row(q_a_gain), wqb, _row(kv_a_gain),
        wkn, wvt, _row(mla_q_gain, MLA_QK_PAD), _row(mla_k_gain, MLA_QK_PAD),
        _row(jnp.tile(swa_q_gain, 2)), _row(jnp.tile(swa_k_gain, 2)), _rope_consts(), tm=tm_in, tk=tq)

    o_mla = _mla_attn(qm, km, vm, tq=tq, hb=4)
    tab = jnp.pad(rel_bias_table.astype(F32).T, ((0, 0), (0, LANES - N_BUCKETS)))
    o_swa = _swa_attn(swa_sinks.astype(F32), qs, ks, vs, positions.reshape(B, S, 1), positions.reshape(B, 1, S), tab)

    h, n2 = _out_proj(o_mla.reshape(T, -1), o_swa.reshape(T, -1), x.reshape(T, D), _row(group_out_gain),
                      w_out.astype(BF16), _row(norm2_gain), tm=tm_out)

    sk = peer_sub_keys.reshape(2 * PEER_HEADS, PEER_NKEYS, PEER_HALF).astype(BF16)
    cnt, a, r2, b = _peer_route(n2, peer_w_q.T.astype(BF16), sk, tr=tr)
    out = _peer_dense(n2, peer_u.astype(BF16), peer_v.T.astype(BF16), cnt, a, r2, b, h, tm=tm_peer, te=te)
    return out.reshape(B, S, D)


def kernel(x, positions, norm1_gain, w_in, q_a_gain, w_q_b, kv_a_gain, w_kv_b, mla_q_gain, mla_k_gain, swa_q_gain, swa_k_gain, swa_sinks, rel_bias_table, group_out_gain, w_out, norm2_gain, peer_w_q, peer_sub_keys, peer_u, peer_v):
    assert norm1_gain.shape[0] == 1, "single-layer trunk"
    return _layer(x, positions, norm1_gain[0], w_in[0], q_a_gain[0], w_q_b[0], kv_a_gain[0], w_kv_b[0],
                  mla_q_gain[0], mla_k_gain[0], swa_q_gain[0], swa_k_gain[0], swa_sinks[0], rel_bias_table,
                  group_out_gain[0], w_out[0], norm2_gain[0], peer_w_q[0], peer_sub_keys[0], peer_u[0], peer_v[0],
                  tm_in=256, tq=512, tm_out=512, tr=256, tm_peer=512, te=1024)
```

```python
import functools
import math

import jax
import jax.numpy as jnp
import numpy as np
from jax import lax
from jax.experimental import pallas as pl
from jax.experimental.pallas import tpu as pltpu

EPS = 1e-6
NEG_INF = -1e30
LANES = 128
VMEM_LIMIT = 56 << 20

MLA_HEADS = 8
MLA_NOPE = 128
MLA_ROPE = 64
MLA_V = 128
MLA_QK = MLA_NOPE + MLA_ROPE
MLA_QK_PAD = 256
MLA_Q_RANK = 512
MLA_KV_RANK = 256
ROPE_THETA = 10000.0

SWA_HEADS = 16
SWA_KV_HEADS = 2
SWA_HD = 64
SWA_GROUP = SWA_HEADS // SWA_KV_HEADS
WINDOW = 128
BLOCK = 128
N_BUCKETS = 32
MAX_DISTANCE = 128

PEER_HEADS = 8
PEER_NKEYS = 128
PEER_TOPK = 16
PEER_HALF = 128

BF16 = jnp.bfloat16
F32 = jnp.float32


def _resident(shape):
    nd = len(shape)
    return pl.BlockSpec(shape, lambda *_: (0,) * nd, pipeline_mode=pl.Buffered(1))


def _rms_scale(x, width):
    return lax.rsqrt(jnp.sum(x * x, axis=-1, keepdims=True) * (1.0 / width) + EPS)


def _dot(a, b):
    return jnp.dot(a, b, preferred_element_type=F32)


def _dot_nt(a, b):
    return lax.dot_general(a, b, (((1,), (1,)), ((), ())), preferred_element_type=F32)


_C_QLAT = 0
_C_KVLAT = _C_QLAT + MLA_Q_RANK
_C_KPE = _C_KVLAT + MLA_KV_RANK
_C_QSWA = _C_KPE + LANES
_C_KSWA = _C_QSWA + SWA_HEADS * SWA_HD
_C_VSWA = _C_KSWA + 2 * LANES
_C_END = _C_VSWA + 2 * LANES


def _rope(x, cos_t, sin_t):
    partner = pltpu.roll(x, 32, axis=1) + pltpu.roll(x, 96, axis=1)
    return x * cos_t + partner * sin_t


def _in_proj_kernel(x_ref, pos_ref, g1_ref, win_ref, qag_ref, wqb_ref, kvag_ref, wkn_ref, wvt_ref,
                    qg_ref, kg_ref, sqg_ref, skg_ref, rc_ref,
                    qm_ref, km_ref, vm_ref, qs_ref, ks_ref, vs_ref):
    x = x_ref[0]
    n1 = x * _rms_scale(x, x.shape[-1]) * g1_ref[...]
    proj = _dot(n1.astype(BF16), win_ref[...])

    q_lat = proj[:, _C_QLAT:_C_QLAT + MLA_Q_RANK]
    ql = q_lat * _rms_scale(q_lat, MLA_Q_RANK) * qag_ref[...]
    q = _dot(ql.astype(BF16), wqb_ref[...])
    kv_lat = proj[:, _C_KVLAT:_C_KVLAT + MLA_KV_RANK]
    kvl = (kv_lat * _rms_scale(kv_lat, MLA_KV_RANK) * kvag_ref[...]).astype(BF16)
    k_nope = _dot(kvl, wkn_ref[...])
    for h in range(MLA_HEADS):
        vm_ref[0, h, 0] = _dot_nt(wvt_ref[h], kvl).astype(BF16)

    pos = pos_ref[0].astype(F32)
    ang = pos * rc_ref[0:1, :]
    cos_t = jnp.cos(ang) * rc_ref[1:2, :]
    sin_t = jnp.sin(ang) * rc_ref[2:3, :]

    q_scale = MLA_QK ** -0.5 * math.log2(math.e)
    for h in range(MLA_HEADS):
        qh = q[:, h * MLA_QK_PAD:(h + 1) * MLA_QK_PAD]
        qn = qh * _rms_scale(qh, MLA_QK) * qg_ref[...]
        qr = _rope(qn[:, LANES:], cos_t, sin_t)
        qm_ref[0, h, :, 0:LANES] = (qn[:, :LANES] * q_scale).astype(BF16)
        qm_ref[0, h, :, LANES:] = (qr * q_scale).astype(BF16)

    kpe = proj[:, _C_KPE:_C_KPE + LANES]
    kpe_ss = jnp.sum(kpe * kpe, axis=-1, keepdims=True)
    kr = _rope(kpe * kg_ref[:, LANES:], cos_t, sin_t)
    for h in range(MLA_HEADS):
        kn = k_nope[:, h * MLA_NOPE:(h + 1) * MLA_NOPE]
        ss = jnp.sum(kn * kn, axis=-1, keepdims=True) + kpe_ss
        r = lax.rsqrt(ss * (1.0 / MLA_QK) + EPS)
        km_ref[0, h, :, 0:LANES] = (kn * r * kg_ref[:, :LANES]).astype(BF16)
        km_ref[0, h, :, LANES:] = (kr * r).astype(BF16)

    lane = lax.broadcasted_iota(jnp.int32, (1, LANES), 1)
    lo = lane < SWA_HD
    s_scale = SWA_HD ** -0.5
    for p in range(SWA_HEADS // 2):
        v = proj[:, _C_QSWA + p * LANES:_C_QSWA + (p + 1) * LANES]
        sq = v * v
        ss_lo = jnp.sum(jnp.where(lo, sq, 0.0), axis=-1, keepdims=True)
        ss_hi = jnp.sum(jnp.where(lo, 0.0, sq), axis=-1, keepdims=True)
        r = jnp.where(lo, lax.rsqrt(ss_lo * (1.0 / SWA_HD) + EPS), lax.rsqrt(ss_hi * (1.0 / SWA_HD) + EPS))
        qs_ref[0, :, p * LANES:(p + 1) * LANES] = (v * r * sqg_ref[...] * s_scale).astype(BF16)
    for g in range(SWA_KV_HEADS):
        v = proj[:, _C_KSWA + g * LANES:_C_KSWA + (g + 1) * LANES]
        r = lax.rsqrt(jnp.sum(v * v, axis=-1, keepdims=True) * (0.5 / SWA_HD) + EPS)
        ks_ref[0, :, g * LANES:(g + 1) * LANES] = (v * r * skg_ref[...]).astype(BF16)
    vs_ref[0] = proj[:, _C_VSWA:_C_END].astype(BF16)


def _in_proj(x, pos_col, g1, win, qag, wqb, kvag, wkn, wvt, qg, kg, sqg, skg, rc, *, tm, tk):
    B, S, D = x.shape
    grid = (B, S // tm)
    per_chunk = tk // tm
    tok = lambda w: pl.BlockSpec((1, tm, w), lambda b, s: (b, s, 0))
    heads = lambda w: pl.BlockSpec((1, MLA_HEADS, tm, w), lambda b, s: (b, 0, s, 0))
    vt_spec = pl.BlockSpec((1, MLA_HEADS, 1, MLA_V, tm), lambda b, s: (b, 0, s // per_chunk, 0, s % per_chunk))
    return pl.pallas_call(
        _in_proj_kernel,
        grid=grid,
        in_specs=[tok(D), tok(1)] + [_resident(a.shape) for a in (g1, win, qag, wqb, kvag, wkn, wvt, qg, kg, sqg, skg, rc)],
        out_specs=[heads(MLA_QK_PAD), heads(MLA_QK_PAD), vt_spec, tok(SWA_HEADS * SWA_HD), tok(2 * LANES), tok(2 * LANES)],
        out_shape=[
            jax.ShapeDtypeStruct((B, MLA_HEADS, S, MLA_QK_PAD), BF16),
            jax.ShapeDtypeStruct((B, MLA_HEADS, S, MLA_QK_PAD), BF16),
            jax.ShapeDtypeStruct((B, MLA_HEADS, S // tk, MLA_V, tk), BF16),
            jax.ShapeDtypeStruct((B, S, SWA_HEADS * SWA_HD), BF16),
            jax.ShapeDtypeStruct((B, S, 2 * LANES), BF16),
            jax.ShapeDtypeStruct((B, S, 2 * LANES), BF16),
        ],
        compiler_params=pltpu.CompilerParams(dimension_semantics=("parallel", "parallel"), vmem_limit_bytes=VMEM_LIMIT),
        name="in_proj",
    )(x, pos_col, g1, win, qag, wqb, kvag, wkn, wvt, qg, kg, sqg, skg, rc)


def _mla_attn_kernel(q_ref, k_ref, vt_ref, o_ref, *, tq):
    qi = pl.program_id(2)
    heads = q_ref.shape[1]

    def chunk(j, carry, masked):
        start = pl.multiple_of(j * tq, tq)
        out = []
        scores = [_dot_nt(k_ref[0, hh, pl.ds(start, tq), :], q_ref[0, hh]) for hh in range(heads)]
        for hh in range(heads):
            m, l, acc = carry[hh]
            st = scores[hh]
            if masked:
                key = lax.broadcasted_iota(jnp.int32, (tq, tq), 0)
                qry = lax.broadcasted_iota(jnp.int32, (tq, tq), 1)
                st = jnp.where(key <= qry, st, NEG_INF)
            m_new = jnp.maximum(m, jnp.max(st, axis=0, keepdims=True))
            alpha = jnp.exp2(m - m_new)
            p = jnp.exp2(st - m_new)
            l = alpha * l + jnp.sum(p, axis=0, keepdims=True)
            acc = alpha * acc + _dot(vt_ref[0, hh, j], p.astype(BF16))
            out.append((m_new, l, acc))
        return tuple(out)

    init = tuple((jnp.full((1, tq), NEG_INF, F32), jnp.zeros((1, tq), F32), jnp.zeros((MLA_V, tq), F32))
                 for _ in range(heads))
    carry = lax.fori_loop(0, qi, lambda j, c: chunk(j, c, False), init)
    for hh, (m, l, acc) in enumerate(chunk(qi, carry, True)):
        o_ref[0, :, hh * MLA_V:(hh + 1) * MLA_V] = (acc / l).T


def _mla_attn(qm, km, vm, *, tq, hb):
    B, H, S, _ = qm.shape
    return pl.pallas_call(
        functools.partial(_mla_attn_kernel, tq=tq),
        grid=(B, H // hb, S // tq),
        in_specs=[
            pl.BlockSpec((1, hb, tq, MLA_QK_PAD), lambda b, h, i: (b, h, i, 0)),
            pl.BlockSpec((1, hb, S, MLA_QK_PAD), lambda b, h, i: (b, h, 0, 0)),
            pl.BlockSpec((1, hb, S // tq, MLA_V, tq), lambda b, h, i: (b, h, 0, 0, 0)),
        ],
        out_specs=pl.BlockSpec((1, tq, hb * MLA_V), lambda b, h, i: (b, i, h)),
        out_shape=jax.ShapeDtypeStruct((B, S, H * MLA_V), F32),
        compiler_params=pltpu.CompilerParams(dimension_semantics=("parallel", "parallel", "arbitrary"),
                                             vmem_limit_bytes=VMEM_LIMIT),
        name="mla_attn",
    )(qm, km, vm)


def _t5_bucket(dist):
    n = jnp.maximum(dist, 0)
    max_exact = N_BUCKETS // 2
    nf = jnp.maximum(n, 1).astype(F32)
    large = max_exact + (jnp.log(nf / max_exact) / math.log(MAX_DISTANCE / max_exact)
                         * (N_BUCKETS - max_exact)).astype(jnp.int32)
    large = jnp.minimum(large, N_BUCKETS - 1)
    return jnp.where(n < max_exact, n, large)


def _swa_attn_kernel(sink_ref, q_ref, kp_ref, kc_ref, vp_ref, vc_ref, pcp_ref, pcc_ref, prow_ref, tab_ref, o_ref):
    n = pl.program_id(1)
    band = 2 * BLOCK
    kb = jnp.concatenate([kp_ref[0], kc_ref[0]], axis=0)
    vb = jnp.concatenate([vp_ref[0], vc_ref[0]], axis=0)
    kpos = jnp.concatenate([pcp_ref[0], pcc_ref[0]], axis=0)
    bucket = _t5_bucket(prow_ref[0] - kpos)

    key = lax.broadcasted_iota(jnp.int32, (band, BLOCK), 0)
    qry = lax.broadcasted_iota(jnp.int32, (band, BLOCK), 1)
    off = qry + BLOCK - key
    valid = (off >= 0) & (off < WINDOW) & ((key >= BLOCK) | (n > 0))

    lo_lane = lax.broadcasted_iota(jnp.int32, (1, LANES), 1) < SWA_HD
    lo_row = lax.broadcasted_iota(jnp.int32, (LANES, 1), 0) < SWA_HD
    zero = jnp.zeros((), BF16)

    pairs = SWA_GROUP // 2
    scores = {}
    for g in range(SWA_KV_HEADS):
        qg = jnp.concatenate([q_ref[0, :, (g * pairs + pi) * LANES:(g * pairs + pi + 1) * LANES]
                              for pi in range(pairs)], axis=0)
        kg = kb[:, g * LANES:(g + 1) * LANES]
        scores[g, 0] = _dot_nt(jnp.where(lo_lane, kg, zero), qg)
        scores[g, 1] = _dot_nt(jnp.where(lo_lane, zero, kg), qg)

    bias = []
    for h in range(SWA_HEADS):
        tab_row = jnp.broadcast_to(tab_ref[h:h + 1, :], (band, LANES))
        bias.append(jnp.take_along_axis(tab_row, bucket, axis=1))

    outs = []
    for g in range(SWA_KV_HEADS):
        probs = ([], [])
        inv = ([], [])
        for pi in range(pairs):
            for half in range(2):
                h = 2 * (g * pairs + pi) + half
                s = scores[g, half][:, pi * BLOCK:(pi + 1) * BLOCK]
                s = jnp.where(valid, s + bias[h], NEG_INF)
                sink = sink_ref[h]
                m = jnp.maximum(jnp.max(s, axis=0, keepdims=True), sink)
                p = jnp.exp(s - m)
                inv[half].append(1.0 / (jnp.sum(p, axis=0, keepdims=True) + jnp.exp(sink - m)))
                probs[half].append(p.astype(BF16))
        vt = vb[:, g * LANES:(g + 1) * LANES].astype(F32).T.astype(BF16)
        ot = (_dot(jnp.where(lo_row, vt, zero), jnp.concatenate(probs[0], axis=1)) * jnp.concatenate(inv[0], axis=1)
              + _dot(jnp.where(lo_row, zero, vt), jnp.concatenate(probs[1], axis=1)) * jnp.concatenate(inv[1], axis=1))
        for pi in range(pairs):
            outs.append(ot[:, pi * BLOCK:(pi + 1) * BLOCK].T)
    o_ref[0] = jnp.concatenate(outs, axis=1)


def _swa_attn(sinks, qs, ks, vs, pos_col, pos_row, tab):
    B, S, _ = qs.shape
    nb = S // BLOCK
    cur = lambda w: pl.BlockSpec((1, BLOCK, w), lambda b, n: (b, n, 0))
    prev = lambda w: pl.BlockSpec((1, BLOCK, w), lambda b, n: (b, jnp.maximum(n - 1, 0), 0))
    return pl.pallas_call(
        _swa_attn_kernel,
        grid=(B, nb),
        in_specs=[
            pl.BlockSpec(memory_space=pltpu.SMEM),
            cur(SWA_HEADS * SWA_HD), prev(2 * LANES), cur(2 * LANES), prev(2 * LANES), cur(2 * LANES),
            prev(1), cur(1),
            pl.BlockSpec((1, 1, BLOCK), lambda b, n: (b, 0, n)),
            _resident(tab.shape),
        ],
        out_specs=cur(SWA_HEADS * SWA_HD),
        out_shape=jax.ShapeDtypeStruct((B, S, SWA_HEADS * SWA_HD), F32),
        compiler_params=pltpu.CompilerParams(dimension_semantics=("parallel", "arbitrary"), vmem_limit_bytes=VMEM_LIMIT),
        name="swa_attn",
    )(sinks, qs, ks, ks, vs, vs, pos_col, pos_col, pos_row, tab)


def _out_proj_kernel(om_ref, os_ref, x_ref, gout_ref, wout_ref, g2_ref, h_ref, n2_ref):
    om = om_ref[...]
    osw = os_ref[...]
    half = om.shape[-1]
    a = om * _rms_scale(om, half) * gout_ref[:, :half]
    b = osw * _rms_scale(osw, half) * gout_ref[:, half:]
    mixed = jnp.concatenate([a.astype(BF16), b.astype(BF16)], axis=1)
    h = x_ref[...] + _dot(mixed, wout_ref[...])
    h_ref[...] = h
    n2_ref[...] = (h * _rms_scale(h, h.shape[-1]) * g2_ref[...]).astype(BF16)


def _out_proj(om, osw, x2, gout, wout, g2, *, tm):
    T, D = x2.shape
    half = om.shape[-1]
    tok = lambda w: pl.BlockSpec((tm, w), lambda t: (t, 0))
    return pl.pallas_call(
        _out_proj_kernel,
        grid=(T // tm,),
        in_specs=[tok(half), tok(half), tok(D), _resident(gout.shape), _resident(wout.shape), _resident(g2.shape)],
        out_specs=[tok(D), tok(D)],
        out_shape=[jax.ShapeDtypeStruct((T, D), F32), jax.ShapeDtypeStruct((T, D), BF16)],
        compiler_params=pltpu.CompilerParams(dimension_semantics=("parallel",), vmem_limit_bytes=VMEM_LIMIT),
        name="out_proj",
    )(om, osw, x2, gout, wout, g2)


_CAND = [(a, b) for a in range(PEER_TOPK) for b in range(PEER_TOPK) if (a + 1) * (b + 1) <= PEER_TOPK]


def _oddeven_merge(lo, hi, r):
    step = r * 2
    if step < hi - lo:
        yield from _oddeven_merge(lo, hi, step)
        yield from _oddeven_merge(lo + r, hi, step)
        yield from [(i, i + r) for i in range(lo + r, hi - r, step)]
    else:
        yield (lo, lo + r)


def _oddeven_merge_sort(lo, hi):
    if hi - lo >= 1:
        mid = lo + (hi - lo) // 2
        yield from _oddeven_merge_sort(lo, mid)
        yield from _oddeven_merge_sort(mid + 1, hi)
        yield from _oddeven_merge(lo, hi, 1)


_SORT16 = tuple(_oddeven_merge_sort(0, PEER_TOPK - 1))
_BITONIC16 = tuple((i, i + s) for s in (8, 4, 2, 1) for i in range(PEER_TOPK) if not i & s)


def _peer_route_kernel(n2_ref, wqt_ref, sk_ref, cnt_ref, a_ref, r2_ref, b_ref,
                       s_scr, rank_scr, vals_scr, n_scr, m_scr, z_scr, *, tr):
    n_lt = tr // LANES
    n_hp = 2 * PEER_HEADS
    qt = lax.dot_general(wqt_ref[...], n2_ref[...], (((1,), (1,)), ((), ())),
                         preferred_element_type=F32).astype(BF16)
    for hp in range(n_hp):
        s_scr[hp] = _dot(sk_ref[hp], qt[hp * PEER_HALF:(hp + 1) * PEER_HALF, :])

    key = lax.broadcasted_iota(jnp.int32, (PEER_NKEYS, LANES), 0).astype(F32)
    head = lax.broadcasted_iota(jnp.int32, (PEER_HEADS, LANES), 0)
    vals_scr[...] = jnp.zeros_like(vals_scr)

    sub = 8
    n_vr = PEER_NKEYS // sub

    def exchange(vals, i, j):
        vals[i], vals[j] = jnp.maximum(vals[i], vals[j]), jnp.minimum(vals[i], vals[j])

    def top16_network(hp, tie):
        h = hp // 2
        p = hp % 2
        for lt in range(n_lt):
            lanes = slice(lt * LANES, (lt + 1) * LANES)
            x = [s_scr[hp, vi * sub:(vi + 1) * sub, lanes] for vi in range(n_vr)]
            top = list(x)
            for i, j in _SORT16:
                exchange(top, i, j)
            for shift in (4, 2, 1):
                other = [pltpu.roll(top[PEER_TOPK - 1 - r], shift, axis=0) for r in range(PEER_TOPK)]
                top = [jnp.maximum(top[r], other[r]) for r in range(PEER_TOPK)]
                for i, j in _BITONIC16:
                    exchange(top, i, j)
            for r in range(PEER_TOPK):
                vals_scr[p, r, :, lanes] = jnp.where(head == h, top[r], vals_scr[p, r, :, lanes])
                if r:
                    tie = jnp.where(top[r - 1] == top[r], 1.0, tie)
            inside = None
            for vi in range(n_vr):
                rank = jnp.zeros((sub, LANES), F32)
                for r in range(PEER_TOPK):
                    rank = jnp.where(top[r] > x[vi], float(r + 1), rank)
                rank_scr[hp, vi * sub:(vi + 1) * sub, lanes] = rank
                ins = jnp.where(x[vi] >= top[PEER_TOPK - 1], 1.0, 0.0)
                inside = ins if inside is None else inside + ins
            for shift in (4, 2, 1):
                inside = inside + pltpu.roll(inside, shift, axis=0)
            tie = jnp.where(inside != float(PEER_TOPK), 1.0, tie)
        return tie

    tie = lax.fori_loop(0, n_hp, top16_network, jnp.zeros((sub, LANES), F32))

    def top16(hp, _):
        h = hp // 2
        p = hp % 2
        for lt in range(n_lt):
            lanes = slice(lt * LANES, (lt + 1) * LANES)
            v = s_scr[hp, :, lanes]
            rank = jnp.full((PEER_NKEYS, LANES), float(PEER_TOPK), F32)
            for r in range(PEER_TOPK):
                m = jnp.max(v, axis=0, keepdims=True)
                first = jnp.min(jnp.where(v == m, key, float(PEER_NKEYS)), axis=0, keepdims=True)
                hit = key == first
                v = jnp.where(hit, -jnp.inf, v)
                rank = jnp.where(hit, float(r), rank)
                vals_scr[p, r, :, lanes] = jnp.where(head == h, m, vals_scr[p, r, :, lanes])
            rank_scr[hp, :, lanes] = rank
        return 0

    @pl.when(jnp.max(tie) > 0.0)
    def _():
        lax.fori_loop(0, n_hp, top16, 0)

    for lt in range(n_lt):
        lanes = slice(lt * LANES, (lt + 1) * LANES)
        v1 = [vals_scr[0, a, :, lanes] for a in range(PEER_TOPK)]
        v2 = [vals_scr[1, b, :, lanes] for b in range(PEER_TOPK)]
        c = [v1[a] + v2[b] for (a, b) in _CAND]
        flat = [float(a * PEER_TOPK + b) for (a, b) in _CAND]
        for _ in range(PEER_TOPK):
            m = functools.reduce(jnp.maximum, c)
            first = functools.reduce(jnp.minimum, [jnp.where(ci == m, fi, 1e9) for ci, fi in zip(c, flat)])
            c = [jnp.where(first == fi, -jnp.inf, ci) for ci, fi in zip(c, flat)]
        e1 = [jnp.exp(v1[a] - v1[0]) for a in range(PEER_TOPK)]
        e2 = [jnp.exp(v2[b] - v2[0]) for b in range(PEER_TOPK)]
        z = jnp.zeros_like(v1[0])
        n_a = [jnp.zeros_like(v1[0]) for _ in range(PEER_TOPK)]
        for ci, (a, b) in zip(c, _CAND):
            taken = ci == -jnp.inf
            z = z + jnp.where(taken, e1[a] * e2[b], 0.0)
            n_a[a] = n_a[a] + jnp.where(taken, 1.0, 0.0)
        for a in range(PEER_TOPK):
            n_scr[a, :, lanes] = n_a[a]
        m_scr[0, :, lanes] = v1[0]
        m_scr[1, :, lanes] = v2[0]
        z_scr[:, lanes] = 1.0 / z

    def spread(h, _):
        rank1 = rank_scr[2 * h]
        cnt = jnp.zeros((PEER_NKEYS, tr), F32)
        for a in range(PEER_TOPK):
            cnt = jnp.where(rank1 == float(a), n_scr[a, pl.ds(h, 1), :], cnt)
        a = jnp.exp(s_scr[2 * h] - m_scr[0, pl.ds(h, 1), :]) * z_scr[pl.ds(h, 1), :]
        for lt in range(n_lt):
            cnt_ref[h, lt] = cnt[:, lt * LANES:(lt + 1) * LANES]
            a_ref[h, lt] = a[:, lt * LANES:(lt + 1) * LANES]
        r2_ref[h] = rank_scr[2 * h + 1].astype(BF16)
        b_ref[h] = jnp.exp(s_scr[2 * h + 1] - m_scr[1, pl.ds(h, 1), :]).astype(BF16)
        return 0

    lax.fori_loop(0, PEER_HEADS, spread, 0)


def _peer_route(n2, wqt, sk, *, tr):
    T, D = n2.shape
    route = pl.BlockSpec((PEER_HEADS, PEER_NKEYS, tr), lambda t: (0, 0, t))
    shape = lambda dt: jax.ShapeDtypeStruct((PEER_HEADS, PEER_NKEYS, T), dt)
    slab = pl.BlockSpec((PEER_HEADS, tr // LANES, PEER_NKEYS, LANES), lambda t: (0, t, 0, 0))
    slab_shape = jax.ShapeDtypeStruct((PEER_HEADS, T // LANES, PEER_NKEYS, LANES), F32)
    return pl.pallas_call(
        functools.partial(_peer_route_kernel, tr=tr),
        grid=(T // tr,),
        in_specs=[pl.BlockSpec((tr, D), lambda t: (t, 0)), _resident(wqt.shape), _resident(sk.shape)],
        out_specs=[slab, slab, route, route],
        out_shape=[slab_shape, slab_shape, shape(BF16), shape(BF16)],
        scratch_shapes=[
            pltpu.VMEM((2 * PEER_HEADS, PEER_NKEYS, tr), F32),
            pltpu.VMEM((2 * PEER_HEADS, PEER_NKEYS, tr), F32),
            pltpu.VMEM((2, PEER_TOPK, PEER_HEADS, tr), F32),
            pltpu.VMEM((PEER_TOPK, PEER_HEADS, tr), F32),
            pltpu.VMEM((2, PEER_HEADS, tr), F32),
            pltpu.VMEM((PEER_HEADS, tr), F32),
        ],
        compiler_params=pltpu.CompilerParams(dimension_semantics=("parallel",), vmem_limit_bytes=VMEM_LIMIT),
        name="peer_route",
    )(n2, wqt, sk)


PACK = 16


def _peer_dense_kernel(n2_ref, u_ref, vt_ref, cnt_ref, a_ref, r2_ref, b_ref, h_ref, o_ref, acc_ref, act_scr, w_scr, *, te):
    e = pl.program_id(1)
    tm = n2_ref.shape[0]
    n_sub = te // PEER_NKEYS

    last = pl.num_programs(1) - 1
    cur = e % 2
    zero = jnp.zeros((), BF16)

    def front_matmul():
        act_scr[...] = _dot_nt(u_ref[...], n2_ref[...])

    def back_matmul():
        return _dot(vt_ref[...], w_scr[1 - cur])

    def gate_tile():
        for ii in range(n_sub):
            i = e * n_sub + ii
            row16 = lambda ref, h: jnp.concatenate(
                [ref[h, lt, pl.ds(i, PACK, stride=0), :] for lt in range(tm // LANES)], axis=1).astype(BF16)
            cnt = [row16(cnt_ref, h) for h in range(PEER_HEADS)]
            a = [row16(a_ref, h) for h in range(PEER_HEADS)]
            for jg in range(PEER_NKEYS // PACK):
                keys = slice(jg * PACK, (jg + 1) * PACK)
                gate = None
                for h in range(PEER_HEADS):
                    term = jnp.where(r2_ref[h, keys, :] < cnt[h], b_ref[h, keys, :], zero) * a[h]
                    gate = term if gate is None else gate + term
                rows = slice(ii * PEER_NKEYS + jg * PACK, ii * PEER_NKEYS + (jg + 1) * PACK)
                w_scr[cur, rows, :] = jax.nn.gelu(act_scr[rows, :]).astype(BF16) * gate

    @pl.when(e == 0)
    def _():
        front_matmul()
        acc_ref[...] = jnp.zeros_like(acc_ref)
        gate_tile()

    @pl.when((e > 0) & (e < last))
    def _():
        front_matmul()
        acc_ref[...] += back_matmul()
        gate_tile()

    @pl.when(e == last)
    def _():
        o_ref[...] = h_ref[...] + (acc_ref[...] + back_matmul()).T


def _peer_dense(n2, u, vt, cnt, a, r2, b, h, *, tm, te):
    T, D = n2.shape
    n_tiles = u.shape[0] // te
    route = pl.BlockSpec((PEER_HEADS, PEER_NKEYS, tm), lambda t, e: (0, 0, t))
    route_once = pl.BlockSpec((PEER_HEADS, tm // LANES, PEER_NKEYS, LANES), lambda t, e: (0, t, 0, 0),
                              pipeline_mode=pl.Buffered(1))
    return pl.pallas_call(
        functools.partial(_peer_dense_kernel, te=te),
        grid=(T // tm, n_tiles + 1),
        in_specs=[
            pl.BlockSpec((tm, D), lambda t, e: (t, 0), pipeline_mode=pl.Buffered(1)),
            pl.BlockSpec((te, D), lambda t, e: (jnp.minimum(e, n_tiles - 1), 0)),
            pl.BlockSpec((D, te), lambda t, e: (0, jnp.maximum(e - 1, 0))),
            route_once, route_once, route, route,
            pl.BlockSpec((tm, D), lambda t, e: (t, 0), pipeline_mode=pl.Buffered(1)),
        ],
        out_specs=pl.BlockSpec((tm, D), lambda t, e: (t, 0)),
        out_shape=jax.ShapeDtypeStruct((T, D), F32),
        scratch_shapes=[pltpu.VMEM((D, tm), F32), pltpu.VMEM((te, tm), F32), pltpu.VMEM((2, te, tm), BF16)],
        compiler_params=pltpu.CompilerParams(dimension_semantics=("parallel", "arbitrary"), vmem_limit_bytes=VMEM_LIMIT),
        name="peer_dense",
    )(n2, u, vt, cnt, a, r2, b, h)


def _row(v, width=None):
    v = v.astype(F32).reshape(1, -1)
    if width is not None and v.shape[1] < width:
        v = jnp.pad(v, ((0, 0), (0, width - v.shape[1])))
    return v


def _rope_consts():
    half = MLA_ROPE // 2
    inv_freq = ROPE_THETA ** (-jnp.arange(half, dtype=F32) / half)
    z = jnp.zeros((LANES - MLA_ROPE,), F32)
    rows = [
        jnp.concatenate([inv_freq, inv_freq, z]),
        jnp.concatenate([jnp.ones((MLA_ROPE,), F32), z]),
        jnp.concatenate([-jnp.ones((half,), F32), jnp.ones((half,), F32), z]),
    ]
    return jnp.pad(jnp.stack(rows), ((0, 5), (0, 0)))


def _layer(x, positions, norm1_gain, w_in, q_a_gain, w_q_b, kv_a_gain, w_kv_b, mla_q_gain, mla_k_gain,
           swa_q_gain, swa_k_gain, swa_sinks, rel_bias_table, group_out_gain, w_out, norm2_gain,
           peer_w_q, peer_sub_keys, peer_u, peer_v, *, tm_in, tq, tm_out, tr, tm_peer, te):
    B, S, D = x.shape
    T = B * S

    zpad = jnp.zeros((D, LANES - MLA_ROPE), w_in.dtype)
    o = np.cumsum((MLA_Q_RANK, MLA_KV_RANK, MLA_ROPE, SWA_HEADS * SWA_HD, SWA_KV_HEADS * SWA_HD)).tolist()
    k_swa, v_swa = w_in[:, o[3]:o[4]], w_in[:, o[4]:]
    dup = lambda w: jnp.concatenate([w[:, :SWA_HD], w[:, :SWA_HD], w[:, SWA_HD:], w[:, SWA_HD:]], axis=1)
    win = jnp.concatenate([w_in[:, :o[2]], zpad, w_in[:, o[2]:o[3]], dup(k_swa), dup(v_swa)], axis=1).astype(BF16)
    wqb = jnp.pad(w_q_b.reshape(MLA_Q_RANK, MLA_HEADS, MLA_QK), ((0, 0), (0, 0), (0, MLA_QK_PAD - MLA_QK)))
    wqb = wqb.reshape(MLA_Q_RANK, MLA_HEADS * MLA_QK_PAD).astype(BF16)

    wkv = w_kv_b.reshape(MLA_KV_RANK, MLA_HEADS, MLA_NOPE + MLA_V)
    wkn = wkv[:, :, :MLA_NOPE].reshape(MLA_KV_RANK, MLA_HEADS * MLA_NOPE).astype(BF16)
    wvt = wkv[:, :, MLA_NOPE:].transpose(1, 2, 0).astype(BF16)

    qm, km, vm, qs, ks, vs = _in_proj(
        x, positions.reshape(B, S, 1), _row(norm1_gain), win, _row(q_a_gain), wqb, _row(kv_a_gain),
        wkn, wvt, _row(mla_q_gain, MLA_QK_PAD), _row(mla_k_gain, MLA_QK_PAD),
        _row(jnp.tile(swa_q_gain, 2)), _row(jnp.tile(swa_k_gain, 2)), _rope_consts(), tm=tm_in, tk=tq)

    o_mla = _mla_attn(qm, km, vm, tq=tq, hb=4)
    tab = jnp.pad(rel_bias_table.astype(F32).T, ((0, 0), (0, LANES - N_BUCKETS)))
    o_swa = _swa_attn(swa_sinks.astype(F32), qs, ks, vs, positions.reshape(B, S, 1), positions.reshape(B, 1, S), tab)

    h, n2 = _out_proj(o_mla.reshape(T, -1), o_swa.reshape(T, -1), x.reshape(T, D), _row(group_out_gain),
                      w_out.astype(BF16), _row(norm2_gain), tm=tm_out)

    sk = peer_sub_keys.reshape(2 * PEER_HEADS, PEER_NKEYS, PEER_HALF).astype(BF16)
    cnt, a, r2, b = _peer_route(n2, peer_w_q.T.astype(BF16), sk, tr=tr)
    out = _peer_dense(n2, peer_u.astype(BF16), peer_v.T.astype(BF16), cnt, a, r2, b, h, tm=tm_peer, te=te)
    return out.reshape(B, S, D)


def kernel(x, positions, norm1_gain, w_in, q_a_gain, w_q_b, kv_a_gain, w_kv_b, mla_q_gain, mla_k_gain, swa_q_gain, swa_k_gain, swa_sinks, rel_bias_table, group_out_gain, w_out, norm2_gain, peer_w_q, peer_sub_keys, peer_u, peer_v):
    assert norm1_gain.shape[0] == 1, "single-layer trunk"
    return _layer(x, positions, norm1_gain[0], w_in[0], q_a_gain[0], w_q_b[0], kv_a_gain[0], w_kv_b[0],
                  mla_q_gain[0], mla_k_gain[0], swa_q_gain[0], swa_k_gain[0], swa_sinks[0], rel_bias_table,
                  group_out_gain[0], w_out[0], norm2_gain[0], peer_w_q[0], peer_sub_keys[0], peer_u[0], peer_v[0],
                  tm_in=256, tq=512, tm_out=512, tr=256, tm_peer=512, te=1024)
```

```python
import functools
import math

import jax
import jax.numpy as jnp
import numpy as np
from jax import lax
from jax.experimental import pallas as pl
from jax.experimental.pallas import tpu as pltpu

EPS = 1e-6
NEG_INF = -1e30
LANES = 128
VMEM_LIMIT = 56 << 20

MLA_HEADS = 8
MLA_NOPE = 128
MLA_ROPE = 64
MLA_V = 128
MLA_QK = MLA_NOPE + MLA_ROPE
MLA_QK_PAD = 256
MLA_Q_RANK = 512
MLA_KV_RANK = 256
ROPE_THETA = 10000.0

SWA_HEADS = 16
SWA_KV_HEADS = 2
SWA_HD = 64
SWA_GROUP = SWA_HEADS // SWA_KV_HEADS
WINDOW = 128
BLOCK = 128
N_BUCKETS = 32
MAX_DISTANCE = 128

PEER_HEADS = 8
PEER_NKEYS = 128
PEER_TOPK = 16
PEER_HALF = 128

BF16 = jnp.bfloat16
F32 = jnp.float32


def _resident(shape):
    nd = len(shape)
    return pl.BlockSpec(shape, lambda *_: (0,) * nd, pipeline_mode=pl.Buffered(1))


def _rms_scale(x, width):
    return lax.rsqrt(jnp.sum(x * x, axis=-1, keepdims=True) * (1.0 / width) + EPS)


def _dot(a, b):
    return jnp.dot(a, b, preferred_element_type=F32)


def _dot_nt(a, b):
    return lax.dot_general(a, b, (((1,), (1,)), ((), ())), preferred_element_type=F32)


_C_QLAT = 0
_C_KVLAT = _C_QLAT + MLA_Q_RANK
_C_KPE = _C_KVLAT + MLA_KV_RANK
_C_QSWA = _C_KPE + LANES
_C_KSWA = _C_QSWA + SWA_HEADS * SWA_HD
_C_VSWA = _C_KSWA + 2 * LANES
_C_END = _C_VSWA + 2 * LANES


def _rope(x, cos_t, sin_t):
    partner = pltpu.roll(x, 32, axis=1) + pltpu.roll(x, 96, axis=1)
    return x * cos_t + partner * sin_t


def _in_proj_kernel(x_ref, pos_ref, g1_ref, win_ref, qag_ref, wqb_ref, kvag_ref, wkn_ref, wvt_ref,
                    qg_ref, kg_ref, sqg_ref, skg_ref, rc_ref,
                    qm_ref, km_ref, vm_ref, qs_ref, ks_ref, vs_ref):
    x = x_ref[0]
    n1 = x * _rms_scale(x, x.shape[-1]) * g1_ref[...]
    proj = _dot(n1.astype(BF16), win_ref[...])

    q_lat = proj[:, _C_QLAT:_C_QLAT + MLA_Q_RANK]
    ql = q_lat * _rms_scale(q_lat, MLA_Q_RANK) * qag_ref[...]
    q = _dot(ql.astype(BF16), wqb_ref[...])
    kv_lat = proj[:, _C_KVLAT:_C_KVLAT + MLA_KV_RANK]
    kvl = (kv_lat * _rms_scale(kv_lat, MLA_KV_RANK) * kvag_ref[...]).astype(BF16)
    k_nope = _dot(kvl, wkn_ref[...])
    for h in range(MLA_HEADS):
        vm_ref[0, h, 0] = _dot_nt(wvt_ref[h], kvl).astype(BF16)

    pos = pos_ref[0].astype(F32)
    ang = pos * rc_ref[0:1, :]
    cos_t = jnp.cos(ang) * rc_ref[1:2, :]
    sin_t = jnp.sin(ang) * rc_ref[2:3, :]

    q_scale = MLA_QK ** -0.5 * math.log2(math.e)
    for h in range(MLA_HEADS):
        qh = q[:, h * MLA_QK_PAD:(h + 1) * MLA_QK_PAD]
        qn = qh * _rms_scale(qh, MLA_QK) * qg_ref[...]
        qr = _rope(qn[:, LANES:], cos_t, sin_t)
        qm_ref[0, h, :, 0:LANES] = (qn[:, :LANES] * q_scale).astype(BF16)
        qm_ref[0, h, :, LANES:] = (qr * q_scale).astype(BF16)

    kpe = proj[:, _C_KPE:_C_KPE + LANES]
    kpe_ss = jnp.sum(kpe * kpe, axis=-1, keepdims=True)
    kr = _rope(kpe * kg_ref[:, LANES:], cos_t, sin_t)
    for h in range(MLA_HEADS):
        kn = k_nope[:, h * MLA_NOPE:(h + 1) * MLA_NOPE]
        ss = jnp.sum(kn * kn, axis=-1, keepdims=True) + kpe_ss
        r = lax.rsqrt(ss * (1.0 / MLA_QK) + EPS)
        km_ref[0, h, :, 0:LANES] = (kn * r * kg_ref[:, :LANES]).astype(BF16)
        km_ref[0, h, :, LANES:] = (kr * r).astype(BF16)

    lane = lax.broadcasted_iota(jnp.int32, (1, LANES), 1)
    lo = lane < SWA_HD
    s_scale = SWA_HD ** -0.5
    for p in range(SWA_HEADS // 2):
        v = proj[:, _C_QSWA + p * LANES:_C_QSWA + (p + 1) * LANES]
        sq = v * v
        ss_lo = jnp.sum(jnp.where(lo, sq, 0.0), axis=-1, keepdims=True)
        ss_hi = jnp.sum(jnp.where(lo, 0.0, sq), axis=-1, keepdims=True)
        r = jnp.where(lo, lax.rsqrt(ss_lo * (1.0 / SWA_HD) + EPS), lax.rsqrt(ss_hi * (1.0 / SWA_HD) + EPS))
        qs_ref[0, :, p * LANES:(p + 1) * LANES] = (v * r * sqg_ref[...] * s_scale).astype(BF16)
    for g in range(SWA_KV_HEADS):
        v = proj[:, _C_KSWA + g * LANES:_C_KSWA + (g + 1) * LANES]
        r = lax.rsqrt(jnp.sum(v * v, axis=-1, keepdims=True) * (0.5 / SWA_HD) + EPS)
        ks_ref[0, :, g * LANES:(g + 1) * LANES] = (v * r * skg_ref[...]).astype(BF16)
    vs_ref[0] = proj[:, _C_VSWA:_C_END].astype(BF16)


def _in_proj(x, pos_col, g1, win, qag, wqb, kvag, wkn, wvt, qg, kg, sqg, skg, rc, *, tm, tk):
    B, S, D = x.shape
    grid = (B, S // tm)
    per_chunk = tk // tm
    tok = lambda w: pl.BlockSpec((1, tm, w), lambda b, s: (b, s, 0))
    heads = lambda w: pl.BlockSpec((1, MLA_HEADS, tm, w), lambda b, s: (b, 0, s, 0))
    vt_spec = pl.BlockSpec((1, MLA_HEADS, 1, MLA_V, tm), lambda b, s: (b, 0, s // per_chunk, 0, s % per_chunk))
    return pl.pallas_call(
        _in_proj_kernel,
        grid=grid,
        in_specs=[tok(D), tok(1)] + [_resident(a.shape) for a in (g1, win, qag, wqb, kvag, wkn, wvt, qg, kg, sqg, skg, rc)],
        out_specs=[heads(MLA_QK_PAD), heads(MLA_QK_PAD), vt_spec, tok(SWA_HEADS * SWA_HD), tok(2 * LANES), tok(2 * LANES)],
        out_shape=[
            jax.ShapeDtypeStruct((B, MLA_HEADS, S, MLA_QK_PAD), BF16),
            jax.ShapeDtypeStruct((B, MLA_HEADS, S, MLA_QK_PAD), BF16),
            jax.ShapeDtypeStruct((B, MLA_HEADS, S // tk, MLA_V, tk), BF16),
            jax.ShapeDtypeStruct((B, S, SWA_HEADS * SWA_HD), BF16),
            jax.ShapeDtypeStruct((B, S, 2 * LANES), BF16),
            jax.ShapeDtypeStruct((B, S, 2 * LANES), BF16),
        ],
        compiler_params=pltpu.CompilerParams(dimension_semantics=("parallel", "parallel"), vmem_limit_bytes=VMEM_LIMIT),
        name="in_proj",
    )(x, pos_col, g1, win, qag, wqb, kvag, wkn, wvt, qg, kg, sqg, skg, rc)


def _mla_attn_kernel(q_ref, k_ref, vt_ref, o_ref, *, tq):
    qi = pl.program_id(2)
    heads = q_ref.shape[1]

    def chunk(j, carry, masked):
        start = pl.multiple_of(j * tq, tq)
        out = []
        scores = [_dot_nt(k_ref[0, hh, pl.ds(start, tq), :], q_ref[0, hh]) for hh in range(heads)]
        for hh in range(heads):
            m, l, acc = carry[hh]
            st = scores[hh]
            if masked:
                key = lax.broadcasted_iota(jnp.int32, (tq, tq), 0)
                qry = lax.broadcasted_iota(jnp.int32, (tq, tq), 1)
                st = jnp.where(key <= qry, st, NEG_INF)
            m_new = jnp.maximum(m, jnp.max(st, axis=0, keepdims=True))
            alpha = jnp.exp2(m - m_new)
            p = jnp.exp2(st - m_new)
            l = alpha * l + jnp.sum(p, axis=0, keepdims=True)
            acc = alpha * acc + _dot(vt_ref[0, hh, j], p.astype(BF16))
            out.append((m_new, l, acc))
        return tuple(out)

    init = tuple((jnp.full((1, tq), NEG_INF, F32), jnp.zeros((1, tq), F32), jnp.zeros((MLA_V, tq), F32))
                 for _ in range(heads))
    carry = lax.fori_loop(0, qi, lambda j, c: chunk(j, c, False), init)
    for hh, (m, l, acc) in enumerate(chunk(qi, carry, True)):
        o_ref[0, :, hh * MLA_V:(hh + 1) * MLA_V] = (acc / l).T


def _mla_attn(qm, km, vm, *, tq, hb):
    B, H, S, _ = qm.shape
    return pl.pallas_call(
        functools.partial(_mla_attn_kernel, tq=tq),
        grid=(B, H // hb, S // tq),
        in_specs=[
            pl.BlockSpec((1, hb, tq, MLA_QK_PAD), lambda b, h, i: (b, h, i, 0)),
            pl.BlockSpec((1, hb, S, MLA_QK_PAD), lambda b, h, i: (b, h, 0, 0)),
            pl.BlockSpec((1, hb, S // tq, MLA_V, tq), lambda b, h, i: (b, h, 0, 0, 0)),
        ],
        out_specs=pl.BlockSpec((1, tq, hb * MLA_V), lambda b, h, i: (b, i, h)),
        out_shape=jax.ShapeDtypeStruct((B, S, H * MLA_V), F32),
        compiler_params=pltpu.CompilerParams(dimension_semantics=("parallel", "parallel", "arbitrary"),
                                             vmem_limit_bytes=VMEM_LIMIT),
        name="mla_attn",
    )(qm, km, vm)


def _t5_bucket(dist):
    n = jnp.maximum(dist, 0)
    max_exact = N_BUCKETS // 2
    nf = jnp.maximum(n, 1).astype(F32)
    large = max_exact + (jnp.log(nf / max_exact) / math.log(MAX_DISTANCE / max_exact)
                         * (N_BUCKETS - max_exact)).astype(jnp.int32)
    large = jnp.minimum(large, N_BUCKETS - 1)
    return jnp.where(n < max_exact, n, large)


def _swa_attn_kernel(sink_ref, q_ref, kp_ref, kc_ref, vp_ref, vc_ref, pcp_ref, pcc_ref, prow_ref, tab_ref, o_ref):
    n = pl.program_id(1)
    band = 2 * BLOCK
    kb = jnp.concatenate([kp_ref[0], kc_ref[0]], axis=0)
    vb = jnp.concatenate([vp_ref[0], vc_ref[0]], axis=0)
    kpos = jnp.concatenate([pcp_ref[0], pcc_ref[0]], axis=0)
    bucket = _t5_bucket(prow_ref[0] - kpos)

    key = lax.broadcasted_iota(jnp.int32, (band, BLOCK), 0)
    qry = lax.broadcasted_iota(jnp.int32, (band, BLOCK), 1)
    off = qry + BLOCK - key
    valid = (off >= 0) & (off < WINDOW) & ((key >= BLOCK) | (n > 0))

    lo_lane = lax.broadcasted_iota(jnp.int32, (1, LANES), 1) < SWA_HD
    lo_row = lax.broadcasted_iota(jnp.int32, (LANES, 1), 0) < SWA_HD
    zero = jnp.zeros((), BF16)

    pairs = SWA_GROUP // 2
    scores = {}
    for g in range(SWA_KV_HEADS):
        qg = jnp.concatenate([q_ref[0, :, (g * pairs + pi) * LANES:(g * pairs + pi + 1) * LANES]
                              for pi in range(pairs)], axis=0)
        kg = kb[:, g * LANES:(g + 1) * LANES]
        scores[g, 0] = _dot_nt(jnp.where(lo_lane, kg, zero), qg)
        scores[g, 1] = _dot_nt(jnp.where(lo_lane, zero, kg), qg)

    bias = []
    for h in range(SWA_HEADS):
        tab_row = jnp.broadcast_to(tab_ref[h:h + 1, :], (band, LANES))
        bias.append(jnp.take_along_axis(tab_row, bucket, axis=1))

    outs = []
    for g in range(SWA_KV_HEADS):
        probs = ([], [])
        inv = ([], [])
        for pi in range(pairs):
            for half in range(2):
                h = 2 * (g * pairs + pi) + half
                s = scores[g, half][:, pi * BLOCK:(pi + 1) * BLOCK]
                s = jnp.where(valid, s + bias[h], NEG_INF)
                sink = sink_ref[h]
                m = jnp.maximum(jnp.max(s, axis=0, keepdims=True), sink)
                p = jnp.exp(s - m)
                inv[half].append(1.0 / (jnp.sum(p, axis=0, keepdims=True) + jnp.exp(sink - m)))
                probs[half].append(p.astype(BF16))
        vt = vb[:, g * LANES:(g + 1) * LANES].astype(F32).T.astype(BF16)
        ot = (_dot(jnp.where(lo_row, vt, zero), jnp.concatenate(probs[0], axis=1)) * jnp.concatenate(inv[0], axis=1)
              + _dot(jnp.where(lo_row, zero, vt), jnp.concatenate(probs[1], axis=1)) * jnp.concatenate(inv[1], axis=1))
        for pi in range(pairs):
            outs.append(ot[:, pi * BLOCK:(pi + 1) * BLOCK].T)
    o_ref[0] = jnp.concatenate(outs, axis=1)


def _swa_attn(sinks, qs, ks, vs, pos_col, pos_row, tab):
    B, S, _ = qs.shape
    nb = S // BLOCK
    cur = lambda w: pl.BlockSpec((1, BLOCK, w), lambda b, n: (b, n, 0))
    prev = lambda w: pl.BlockSpec((1, BLOCK, w), lambda b, n: (b, jnp.maximum(n - 1, 0), 0))
    return pl.pallas_call(
        _swa_attn_kernel,
        grid=(B, nb),
        in_specs=[
            pl.BlockSpec(memory_space=pltpu.SMEM),
            cur(SWA_HEADS * SWA_HD), prev(2 * LANES), cur(2 * LANES), prev(2 * LANES), cur(2 * LANES),
            prev(1), cur(1),
            pl.BlockSpec((1, 1, BLOCK), lambda b, n: (b, 0, n)),
            _resident(tab.shape),
        ],
        out_specs=cur(SWA_HEADS * SWA_HD),
        out_shape=jax.ShapeDtypeStruct((B, S, SWA_HEADS * SWA_HD), F32),
        compiler_params=pltpu.CompilerParams(dimension_semantics=("parallel", "arbitrary"), vmem_limit_bytes=VMEM_LIMIT),
        name="swa_attn",
    )(sinks, qs, ks, ks, vs, vs, pos_col, pos_col, pos_row, tab)


def _out_proj_kernel(om_ref, os_ref, x_ref, gout_ref, wout_ref, g2_ref, h_ref, n2_ref):
    om = om_ref[...]
    osw = os_ref[...]
    half = om.shape[-1]
    a = om * _rms_scale(om, half) * gout_ref[:, :half]
    b = osw * _rms_scale(osw, half) * gout_ref[:, half:]
    mixed = jnp.concatenate([a.astype(BF16), b.astype(BF16)], axis=1)
    h = x_ref[...] + _dot(mixed, wout_ref[...])
    h_ref[...] = h
    n2_ref[...] = (h * _rms_scale(h, h.shape[-1]) * g2_ref[...]).astype(BF16)


def _out_proj(om, osw, x2, gout, wout, g2, *, tm):
    T, D = x2.shape
    half = om.shape[-1]
    tok = lambda w: pl.BlockSpec((tm, w), lambda t: (t, 0))
    return pl.pallas_call(
        _out_proj_kernel,
        grid=(T // tm,),
        in_specs=[tok(half), tok(half), tok(D), _resident(gout.shape), _resident(wout.shape), _resident(g2.shape)],
        out_specs=[tok(D), tok(D)],
        out_shape=[jax.ShapeDtypeStruct((T, D), F32), jax.ShapeDtypeStruct((T, D), BF16)],
        compiler_params=pltpu.CompilerParams(dimension_semantics=("parallel",), vmem_limit_bytes=VMEM_LIMIT),
        name="out_proj",
    )(om, osw, x2, gout, wout, g2)


_CAND = [(a, b) for a in range(PEER_TOPK) for b in range(PEER_TOPK) if (a + 1) * (b + 1) <= PEER_TOPK]


def _oddeven_merge(lo, hi, r):
    step = r * 2
    if step < hi - lo:
        yield from _oddeven_merge(lo, hi, step)
        yield from _oddeven_merge(lo + r, hi, step)
        yield from [(i, i + r) for i in range(lo + r, hi - r, step)]
    else:
        yield (lo, lo + r)


def _oddeven_merge_sort(lo, hi):
    if hi - lo >= 1:
        mid = lo + (hi - lo) // 2
        yield from _oddeven_merge_sort(lo, mid)
        yield from _oddeven_merge_sort(mid + 1, hi)
        yield from _oddeven_merge(lo, hi, 1)


_SORT16 = tuple(_oddeven_merge_sort(0, PEER_TOPK - 1))
_BITONIC16 = tuple((i, i + s) for s in (8, 4, 2, 1) for i in range(PEER_TOPK) if not i & s)


def _peer_route_kernel(n2_ref, wqt_ref, sk_ref, cnt_ref, a_ref, r2_ref, b_ref,
                       s_scr, rank_scr, vals_scr, n_scr, m_scr, z_scr, tie_scr, *, tr):
    n_lt = tr // LANES
    n_hp = 2 * PEER_HEADS
    qt = lax.dot_general(wqt_ref[...], n2_ref[...], (((1,), (1,)), ((), ())),
                         preferred_element_type=F32).astype(BF16)
    for hp in range(n_hp):
        s_scr[hp] = _dot(sk_ref[hp], qt[hp * PEER_HALF:(hp + 1) * PEER_HALF, :])

    key = lax.broadcasted_iota(jnp.int32, (PEER_NKEYS, LANES), 0).astype(F32)
    head = lax.broadcasted_iota(jnp.int32, (PEER_HEADS, LANES), 0)
    vals_scr[...] = jnp.zeros_like(vals_scr)

    sub = 8
    n_vr = PEER_NKEYS // sub

    def exchange(vals, i, j):
        vals[i], vals[j] = jnp.maximum(vals[i], vals[j]), jnp.minimum(vals[i], vals[j])

    def top16_network(hp, any_tie):
        h = hp // 2
        p = hp % 2
        tie = jnp.zeros((sub, LANES), F32)
        for lt in range(n_lt):
            lanes = slice(lt * LANES, (lt + 1) * LANES)
            x = [s_scr[hp, vi * sub:(vi + 1) * sub, lanes] for vi in range(n_vr)]
            top = list(x)
            for i, j in _SORT16:
                exchange(top, i, j)
            for shift in (4, 2, 1):
                other = [pltpu.roll(top[PEER_TOPK - 1 - r], shift, axis=0) for r in range(PEER_TOPK)]
                top = [jnp.maximum(top[r], other[r]) for r in range(PEER_TOPK)]
                for i, j in _BITONIC16:
                    exchange(top, i, j)
            for r in range(PEER_TOPK):
                vals_scr[p, r, :, lanes] = jnp.where(head == h, top[r], vals_scr[p, r, :, lanes])
                if r:
                    tie = jnp.where(top[r - 1] == top[r], 1.0, tie)
            inside = None
            for vi in range(n_vr):
                rank = jnp.zeros((sub, LANES), F32)
                for r in range(PEER_TOPK):
                    rank = jnp.where(top[r] > x[vi], float(r + 1), rank)
                rank_scr[hp, vi * sub:(vi + 1) * sub, lanes] = rank
                ins = jnp.where(x[vi] >= top[PEER_TOPK - 1], 1.0, 0.0)
                inside = ins if inside is None else inside + ins
            for shift in (4, 2, 1):
                inside = inside + pltpu.roll(inside, shift, axis=0)
            tie = jnp.where(inside != float(PEER_TOPK), 1.0, tie)
        tie_scr[hp] = tie
        return jnp.maximum(any_tie, tie)

    any_tie = lax.fori_loop(0, n_hp, top16_network, jnp.zeros((sub, LANES), F32))

    def top16(hp, _):
        h = hp // 2
        p = hp % 2

        @pl.when(jnp.max(tie_scr[hp]) > 0.0)
        def _():
            for lt in range(n_lt):
                lanes = slice(lt * LANES, (lt + 1) * LANES)
                v = s_scr[hp, :, lanes]
                rank = jnp.full((PEER_NKEYS, LANES), float(PEER_TOPK), F32)
                for r in range(PEER_TOPK):
                    m = jnp.max(v, axis=0, keepdims=True)
                    first = jnp.min(jnp.where(v == m, key, float(PEER_NKEYS)), axis=0, keepdims=True)
                    hit = key == first
                    v = jnp.where(hit, -jnp.inf, v)
                    rank = jnp.where(hit, float(r), rank)
                    vals_scr[p, r, :, lanes] = jnp.where(head == h, m, vals_scr[p, r, :, lanes])
                rank_scr[hp, :, lanes] = rank
        return 0

    @pl.when(jnp.max(any_tie) > 0.0)
    def _():
        lax.fori_loop(0, n_hp, top16, 0)

    for lt in range(n_lt):
        lanes = slice(lt * LANES, (lt + 1) * LANES)
        v1 = [vals_scr[0, a, :, lanes] for a in range(PEER_TOPK)]
        v2 = [vals_scr[1, b, :, lanes] for b in range(PEER_TOPK)]
        c = [v1[a] + v2[b] for (a, b) in _CAND]
        flat = [float(a * PEER_TOPK + b) for (a, b) in _CAND]
        for _ in range(PEER_TOPK):
            m = functools.reduce(jnp.maximum, c)
            first = functools.reduce(jnp.minimum, [jnp.where(ci == m, fi, 1e9) for ci, fi in zip(c, flat)])
            c = [jnp.where(first == fi, -jnp.inf, ci) for ci, fi in zip(c, flat)]
        e1 = [jnp.exp(v1[a] - v1[0]) for a in range(PEER_TOPK)]
        e2 = [jnp.exp(v2[b] - v2[0]) for b in range(PEER_TOPK)]
        z = jnp.zeros_like(v1[0])
        n_a = [jnp.zeros_like(v1[0]) for _ in range(PEER_TOPK)]
        for ci, (a, b) in zip(c, _CAND):
            taken = ci == -jnp.inf
            z = z + jnp.where(taken, e1[a] * e2[b], 0.0)
            n_a[a] = n_a[a] + jnp.where(taken, 1.0, 0.0)
        for a in range(PEER_TOPK):
            n_scr[a, :, lanes] = n_a[a]
        m_scr[0, :, lanes] = v1[0]
        m_scr[1, :, lanes] = v2[0]
        z_scr[:, lanes] = 1.0 / z

    def spread(h, _):
        rank1 = rank_scr[2 * h]
        cnt = jnp.zeros((PEER_NKEYS, tr), F32)
        for a in range(PEER_TOPK):
            cnt = jnp.where(rank1 == float(a), n_scr[a, pl.ds(h, 1), :], cnt)
        a = jnp.exp(s_scr[2 * h] - m_scr[0, pl.ds(h, 1), :]) * z_scr[pl.ds(h, 1), :]
        for lt in range(n_lt):
            cnt_ref[h, lt] = cnt[:, lt * LANES:(lt + 1) * LANES]
            a_ref[h, lt] = a[:, lt * LANES:(lt + 1) * LANES]
        r2_ref[h] = rank_scr[2 * h + 1].astype(BF16)
        b_ref[h] = jnp.exp(s_scr[2 * h + 1] - m_scr[1, pl.ds(h, 1), :]).astype(BF16)
        return 0

    lax.fori_loop(0, PEER_HEADS, spread, 0)


def _peer_route(n2, wqt, sk, *, tr):
    T, D = n2.shape
    route = pl.BlockSpec((PEER_HEADS, PEER_NKEYS, tr), lambda t: (0, 0, t))
    shape = lambda dt: jax.ShapeDtypeStruct((PEER_HEADS, PEER_NKEYS, T), dt)
    slab = pl.BlockSpec((PEER_HEADS, tr // LANES, PEER_NKEYS, LANES), lambda t: (0, t, 0, 0))
    slab_shape = jax.ShapeDtypeStruct((PEER_HEADS, T // LANES, PEER_NKEYS, LANES), F32)
    return pl.pallas_call(
        functools.partial(_peer_route_kernel, tr=tr),
        grid=(T // tr,),
        in_specs=[pl.BlockSpec((tr, D), lambda t: (t, 0)), _resident(wqt.shape), _resident(sk.shape)],
        out_specs=[slab, slab, route, route],
        out_shape=[slab_shape, slab_shape, shape(BF16), shape(BF16)],
        scratch_shapes=[
            pltpu.VMEM((2 * PEER_HEADS, PEER_NKEYS, tr), F32),
            pltpu.VMEM((2 * PEER_HEADS, PEER_NKEYS, tr), F32),
            pltpu.VMEM((2, PEER_TOPK, PEER_HEADS, tr), F32),
            pltpu.VMEM((PEER_TOPK, PEER_HEADS, tr), F32),
            pltpu.VMEM((2, PEER_HEADS, tr), F32),
            pltpu.VMEM((PEER_HEADS, tr), F32),
            pltpu.VMEM((2 * PEER_HEADS, 8, LANES), F32),
        ],
        compiler_params=pltpu.CompilerParams(dimension_semantics=("parallel",), vmem_limit_bytes=VMEM_LIMIT),
        name="peer_route",
    )(n2, wqt, sk)


PACK = 16


def _peer_dense_kernel(n2_ref, u_ref, vt_ref, cnt_ref, a_ref, r2_ref, b_ref, h_ref, o_ref, acc_ref, act_scr, w_scr, *, te):
    e = pl.program_id(1)
    tm = n2_ref.shape[0]
    n_sub = te // PEER_NKEYS

    last = pl.num_programs(1) - 1
    cur = e % 2
    zero = jnp.zeros((), BF16)

    def front_matmul():
        act_scr[...] = _dot_nt(u_ref[...], n2_ref[...])

    def back_matmul():
        return _dot(vt_ref[...], w_scr[1 - cur])

    def gate_tile():
        for ii in range(n_sub):
            i = e * n_sub + ii
            row16 = lambda ref, h: jnp.concatenate(
                [ref[h, lt, pl.ds(i, PACK, stride=0), :] for lt in range(tm // LANES)], axis=1).astype(BF16)
            cnt = [row16(cnt_ref, h) for h in range(PEER_HEADS)]
            a = [row16(a_ref, h) for h in range(PEER_HEADS)]
            for jg in range(PEER_NKEYS // PACK):
                keys = slice(jg * PACK, (jg + 1) * PACK)
                gate = None
                for h in range(PEER_HEADS):
                    term = jnp.where(r2_ref[h, keys, :] < cnt[h], b_ref[h, keys, :], zero) * a[h]
                    gate = term if gate is None else gate + term
                rows = slice(ii * PEER_NKEYS + jg * PACK, ii * PEER_NKEYS + (jg + 1) * PACK)
                w_scr[cur, rows, :] = jax.nn.gelu(act_scr[rows, :]).astype(BF16) * gate

    @pl.when(e == 0)
    def _():
        front_matmul()
        acc_ref[...] = jnp.zeros_like(acc_ref)
        gate_tile()

    @pl.when((e > 0) & (e < last))
    def _():
        front_matmul()
        acc_ref[...] += back_matmul()
        gate_tile()

    @pl.when(e == last)
    def _():
        o_ref[...] = h_ref[...] + (acc_ref[...] + back_matmul()).T


def _peer_dense(n2, u, vt, cnt, a, r2, b, h, *, tm, te):
    T, D = n2.shape
    n_tiles = u.shape[0] // te
    route = pl.BlockSpec((PEER_HEADS, PEER_NKEYS, tm), lambda t, e: (0, 0, t))
    route_once = pl.BlockSpec((PEER_HEADS, tm // LANES, PEER_NKEYS, LANES), lambda t, e: (0, t, 0, 0),
                              pipeline_mode=pl.Buffered(1))
    return pl.pallas_call(
        functools.partial(_peer_dense_kernel, te=te),
        grid=(T // tm, n_tiles + 1),
        in_specs=[
            pl.BlockSpec((tm, D), lambda t, e: (t, 0), pipeline_mode=pl.Buffered(1)),
            pl.BlockSpec((te, D), lambda t, e: (jnp.minimum(e, n_tiles - 1), 0)),
            pl.BlockSpec((D, te), lambda t, e: (0, jnp.maximum(e - 1, 0))),
            route_once, route_once, route, route,
            pl.BlockSpec((tm, D), lambda t, e: (t, 0), pipeline_mode=pl.Buffered(1)),
        ],
        out_specs=pl.BlockSpec((tm, D), lambda t, e: (t, 0)),
        out_shape=jax.ShapeDtypeStruct((T, D), F32),
        scratch_shapes=[pltpu.VMEM((D, tm), F32), pltpu.VMEM((te, tm), F32), pltpu.VMEM((2, te, tm), BF16)],
        compiler_params=pltpu.CompilerParams(dimension_semantics=("parallel", "arbitrary"), vmem_limit_bytes=VMEM_LIMIT),
        name="peer_dense",
    )(n2, u, vt, cnt, a, r2, b, h)


def _row(v, width=None):
    v = v.astype(F32).reshape(1, -1)
    if width is not None and v.shape[1] < width:
        v = jnp.pad(v, ((0, 0), (0, width - v.shape[1])))
    return v


def _rope_consts():
    half = MLA_ROPE // 2
    inv_freq = ROPE_THETA ** (-jnp.arange(half, dtype=F32) / half)
    z = jnp.zeros((LANES - MLA_ROPE,), F32)
    rows = [
        jnp.concatenate([inv_freq, inv_freq, z]),
        jnp.concatenate([jnp.ones((MLA_ROPE,), F32), z]),
        jnp.concatenate([-jnp.ones((half,), F32), jnp.ones((half,), F32), z]),
    ]
    return jnp.pad(jnp.stack(rows), ((0, 5), (0, 0)))


def _layer(x, positions, norm1_gain, w_in, q_a_gain, w_q_b, kv_a_gain, w_kv_b, mla_q_gain, mla_k_gain,
           swa_q_gain, swa_k_gain, swa_sinks, rel_bias_table, group_out_gain, w_out, norm2_gain,
           peer_w_q, peer_sub_keys, peer_u, peer_v, *, tm_in, tq, tm_out, tr, tm_peer, te):
    B, S, D = x.shape
    T = B * S

    zpad = jnp.zeros((D, LANES - MLA_ROPE), w_in.dtype)
    o = np.cumsum((MLA_Q_RANK, MLA_KV_RANK, MLA_ROPE, SWA_HEADS * SWA_HD, SWA_KV_HEADS * SWA_HD)).tolist()
    k_swa, v_swa = w_in[:, o[3]:o[4]], w_in[:, o[4]:]
    dup = lambda w: jnp.concatenate([w[:, :SWA_HD], w[:, :SWA_HD], w[:, SWA_HD:], w[:, SWA_HD:]], axis=1)
    win = jnp.concatenate([w_in[:, :o[2]], zpad, w_in[:, o[2]:o[3]], dup(k_swa), dup(v_swa)], axis=1).astype(BF16)
    wqb = jnp.pad(w_q_b.reshape(MLA_Q_RANK, MLA_HEADS, MLA_QK), ((0, 0), (0, 0), (0, MLA_QK_PAD - MLA_QK)))
    wqb = wqb.reshape(MLA_Q_RANK, MLA_HEADS * MLA_QK_PAD).astype(BF16)

    wkv = w_kv_b.reshape(MLA_KV_RANK, MLA_HEADS, MLA_NOPE + MLA_V)
    wkn = wkv[:, :, :MLA_NOPE].reshape(MLA_KV_RANK, MLA_HEADS * MLA_NOPE).astype(BF16)
    wvt = wkv[:, :, MLA_NOPE:].transpose(1, 2, 0).astype(BF16)

    qm, km, vm, qs, ks, vs = _in_proj(
        x, positions.reshape(B, S, 1), _row(norm1_gain), win, _row(q_a_gain), wqb, _row(kv_a_gain),
        wkn, wvt, _row(mla_q_gain, MLA_QK_PAD), _row(mla_k_gain, MLA_QK_PAD),
        _row(jnp.tile(swa_q_gain, 2)), _row(jnp.tile(swa_k_gain, 2)), _rope_consts(), tm=tm_in, tk=tq)

    o_mla = _mla_attn(qm, km, vm, tq=tq, hb=4)
    tab = jnp.pad(rel_bias_table.astype(F32).T, ((0, 0), (0, LANES - N_BUCKETS)))
    o_swa = _swa_attn(swa_sinks.astype(F32), qs, ks, vs, positions.reshape(B, S, 1), positions.reshape(B, 1, S), tab)

    h, n2 = _out_proj(o_mla.reshape(T, -1), o_swa.reshape(T, -1), x.reshape(T, D), _row(group_out_gain),
                      w_out.astype(BF16), _row(norm2_gain), tm=tm_out)

    sk = peer_sub_keys.reshape(2 * PEER_HEADS, PEER_NKEYS, PEER_HALF).astype(BF16)
    cnt, a, r2, b = _peer_route(n2, peer_w_q.T.astype(BF16), sk, tr=tr)
    out = _peer_dense(n2, peer_u.astype(BF16), peer_v.T.astype(BF16), cnt, a, r2, b, h, tm=tm_peer, te=te)
    return out.reshape(B, S, D)


def kernel(x, positions, norm1_gain, w_in, q_a_gain, w_q_b, kv_a_gain, w_kv_b, mla_q_gain, mla_k_gain, swa_q_gain, swa_k_gain, swa_sinks, rel_bias_table, group_out_gain, w_out, norm2_gain, peer_w_q, peer_sub_keys, peer_u, peer_v):
    assert norm1_gain.shape[0] == 1, "single-layer trunk"
    return _layer(x, positions, norm1_gain[0], w_in[0], q_a_gain[0], w_q_b[0], kv_a_gain[0], w_kv_b[0],
                  mla_q_gain[0], mla_k_gain[0], swa_q_gain[0], swa_k_gain[0], swa_sinks[0], rel_bias_table,
                  group_out_gain[0], w_out[0], norm2_gain[0], peer_w_q[0], peer_sub_keys[0], peer_u[0], peer_v[0],
                  tm_in=512, tq=512, tm_out=512, tr=512, tm_peer=512, te=1024)
```

```python
import functools
import math

import jax
import jax.numpy as jnp
import numpy as np
from jax import lax
from jax.experimental import pallas as pl
from jax.experimental.pallas import tpu as pltpu

EPS = 1e-6
NEG_INF = -1e30
LANES = 128
VMEM_LIMIT = 58 << 20

MLA_HEADS = 8
MLA_NOPE = 128
MLA_ROPE = 64
MLA_V = 128
MLA_QK = MLA_NOPE + MLA_ROPE
MLA_QK_PAD = 256
MLA_Q_RANK = 512
MLA_KV_RANK = 256
ROPE_THETA = 10000.0

SWA_HEADS = 16
SWA_KV_HEADS = 2
SWA_HD = 64
SWA_GROUP = SWA_HEADS // SWA_KV_HEADS
WINDOW = 128
BLOCK = 128
N_BUCKETS = 32
MAX_DISTANCE = 128

PEER_HEADS = 8
PEER_NKEYS = 128
PEER_TOPK = 16
PEER_HALF = 128

BF16 = jnp.bfloat16
F32 = jnp.float32


def _resident(shape):
    nd = len(shape)
    return pl.BlockSpec(shape, lambda *_: (0,) * nd, pipeline_mode=pl.Buffered(1))


def _rms_scale(x, width):
    return lax.rsqrt(jnp.sum(x * x, axis=-1, keepdims=True) * (1.0 / width) + EPS)


def _dot(a, b):
    return jnp.dot(a, b, preferred_element_type=F32)


def _dot_nt(a, b):
    return lax.dot_general(a, b, (((1,), (1,)), ((), ())), preferred_element_type=F32)


_C_QLAT = 0
_C_KVLAT = _C_QLAT + MLA_Q_RANK
_C_KPE = _C_KVLAT + MLA_KV_RANK
_C_QSWA = _C_KPE + LANES
_C_KSWA = _C_QSWA + SWA_HEADS * SWA_HD
_C_VSWA = _C_KSWA + 2 * LANES
_C_END = _C_VSWA + 2 * LANES


def _rope(x, cos_t, sin_t):
    partner = pltpu.roll(x, 32, axis=1) + pltpu.roll(x, 96, axis=1)
    return x * cos_t + partner * sin_t


def _in_proj_kernel(x_ref, pos_ref, g1_ref, win_ref, qag_ref, wqb_ref, kvag_ref, wkn_ref, wvt_ref,
                    qg_ref, kg_ref, sqg_ref, skg_ref, rc_ref,
                    qm_ref, km_ref, vm_ref, qs_ref, ks_ref, vs_ref):
    x = x_ref[0]
    n1 = x * _rms_scale(x, x.shape[-1]) * g1_ref[...]
    proj = _dot(n1.astype(BF16), win_ref[...])

    q_lat = proj[:, _C_QLAT:_C_QLAT + MLA_Q_RANK]
    ql = q_lat * _rms_scale(q_lat, MLA_Q_RANK) * qag_ref[...]
    q = _dot(ql.astype(BF16), wqb_ref[...])
    kv_lat = proj[:, _C_KVLAT:_C_KVLAT + MLA_KV_RANK]
    kvl = (kv_lat * _rms_scale(kv_lat, MLA_KV_RANK) * kvag_ref[...]).astype(BF16)
    k_nope = _dot(kvl, wkn_ref[...])
    for h in range(MLA_HEADS):
        vm_ref[0, h, 0] = _dot_nt(wvt_ref[h], kvl).astype(BF16)

    pos = pos_ref[0].astype(F32)
    ang = pos * rc_ref[0:1, :]
    cos_t = jnp.cos(ang) * rc_ref[1:2, :]
    sin_t = jnp.sin(ang) * rc_ref[2:3, :]

    q_scale = MLA_QK ** -0.5 * math.log2(math.e)
    for h in range(MLA_HEADS):
        qh = q[:, h * MLA_QK_PAD:(h + 1) * MLA_QK_PAD]
        qn = qh * _rms_scale(qh, MLA_QK) * qg_ref[...]
        qr = _rope(qn[:, LANES:], cos_t, sin_t)
        qm_ref[0, h, :, 0:LANES] = (qn[:, :LANES] * q_scale).astype(BF16)
        qm_ref[0, h, :, LANES:] = (qr * q_scale).astype(BF16)

    kpe = proj[:, _C_KPE:_C_KPE + LANES]
    kpe_ss = jnp.sum(kpe * kpe, axis=-1, keepdims=True)
    kr = _rope(kpe * kg_ref[:, LANES:], cos_t, sin_t)
    for h in range(MLA_HEADS):
        kn = k_nope[:, h * MLA_NOPE:(h + 1) * MLA_NOPE]
        ss = jnp.sum(kn * kn, axis=-1, keepdims=True) + kpe_ss
        r = lax.rsqrt(ss * (1.0 / MLA_QK) + EPS)
        km_ref[0, h, :, 0:LANES] = (kn * r * kg_ref[:, :LANES]).astype(BF16)
        km_ref[0, h, :, LANES:] = (kr * r).astype(BF16)

    lane = lax.broadcasted_iota(jnp.int32, (1, LANES), 1)
    lo = lane < SWA_HD
    s_scale = SWA_HD ** -0.5
    for p in range(SWA_HEADS // 2):
        v = proj[:, _C_QSWA + p * LANES:_C_QSWA + (p + 1) * LANES]
        sq = v * v
        ss_lo = jnp.sum(jnp.where(lo, sq, 0.0), axis=-1, keepdims=True)
        ss_hi = jnp.sum(jnp.where(lo, 0.0, sq), axis=-1, keepdims=True)
        r = jnp.where(lo, lax.rsqrt(ss_lo * (1.0 / SWA_HD) + EPS), lax.rsqrt(ss_hi * (1.0 / SWA_HD) + EPS))
        qs_ref[0, :, p * LANES:(p + 1) * LANES] = (v * r * sqg_ref[...] * s_scale).astype(BF16)
    for g in range(SWA_KV_HEADS):
        v = proj[:, _C_KSWA + g * LANES:_C_KSWA + (g + 1) * LANES]
        r = lax.rsqrt(jnp.sum(v * v, axis=-1, keepdims=True) * (0.5 / SWA_HD) + EPS)
        ks_ref[0, :, g * LANES:(g + 1) * LANES] = (v * r * skg_ref[...]).astype(BF16)
    vs_ref[0] = proj[:, _C_VSWA:_C_END].astype(BF16)


def _in_proj(x, pos_col, g1, win, qag, wqb, kvag, wkn, wvt, qg, kg, sqg, skg, rc, *, tm, tk):
    B, S, D = x.shape
    grid = (B, S // tm)
    per_chunk = tk // tm
    tok = lambda w: pl.BlockSpec((1, tm, w), lambda b, s: (b, s, 0))
    heads = lambda w: pl.BlockSpec((1, MLA_HEADS, tm, w), lambda b, s: (b, 0, s, 0))
    vt_spec = pl.BlockSpec((1, MLA_HEADS, 1, MLA_V, tm), lambda b, s: (b, 0, s // per_chunk, 0, s % per_chunk))
    return pl.pallas_call(
        _in_proj_kernel,
        grid=grid,
        in_specs=[tok(D), tok(1)] + [_resident(a.shape) for a in (g1, win, qag, wqb, kvag, wkn, wvt, qg, kg, sqg, skg, rc)],
        out_specs=[heads(MLA_QK_PAD), heads(MLA_QK_PAD), vt_spec, tok(SWA_HEADS * SWA_HD), tok(2 * LANES), tok(2 * LANES)],
        out_shape=[
            jax.ShapeDtypeStruct((B, MLA_HEADS, S, MLA_QK_PAD), BF16),
            jax.ShapeDtypeStruct((B, MLA_HEADS, S, MLA_QK_PAD), BF16),
            jax.ShapeDtypeStruct((B, MLA_HEADS, S // tk, MLA_V, tk), BF16),
            jax.ShapeDtypeStruct((B, S, SWA_HEADS * SWA_HD), BF16),
            jax.ShapeDtypeStruct((B, S, 2 * LANES), BF16),
            jax.ShapeDtypeStruct((B, S, 2 * LANES), BF16),
        ],
        compiler_params=pltpu.CompilerParams(dimension_semantics=("parallel", "parallel"), vmem_limit_bytes=VMEM_LIMIT),
        name="in_proj",
    )(x, pos_col, g1, win, qag, wqb, kvag, wkn, wvt, qg, kg, sqg, skg, rc)


def _mla_attn_kernel(q_ref, k_ref, vt_ref, o_ref, *, tq):
    qi = pl.program_id(2)
    heads = q_ref.shape[1]

    def chunk(j, carry, masked):
        start = pl.multiple_of(j * tq, tq)
        out = []
        scores = [_dot_nt(k_ref[0, hh, pl.ds(start, tq), :], q_ref[0, hh]) for hh in range(heads)]
        for hh in range(heads):
            m, l, acc = carry[hh]
            st = scores[hh]
            if masked:
                key = lax.broadcasted_iota(jnp.int32, (tq, tq), 0)
                qry = lax.broadcasted_iota(jnp.int32, (tq, tq), 1)
                st = jnp.where(key <= qry, st, NEG_INF)
            m_new = jnp.maximum(m, jnp.max(st, axis=0, keepdims=True))
            alpha = jnp.exp2(m - m_new)
            p = jnp.exp2(st - m_new)
            l = alpha * l + jnp.sum(p, axis=0, keepdims=True)
            acc = alpha * acc + _dot(vt_ref[0, hh, j], p.astype(BF16))
            out.append((m_new, l, acc))
        return tuple(out)

    init = tuple((jnp.full((1, tq), NEG_INF, F32), jnp.zeros((1, tq), F32), jnp.zeros((MLA_V, tq), F32))
                 for _ in range(heads))
    carry = lax.fori_loop(0, qi, lambda j, c: chunk(j, c, False), init)
    for hh, (m, l, acc) in enumerate(chunk(qi, carry, True)):
        o_ref[0, :, hh * MLA_V:(hh + 1) * MLA_V] = (acc / l).T


def _mla_attn(qm, km, vm, *, tq, hb):
    B, H, S, _ = qm.shape
    return pl.pallas_call(
        functools.partial(_mla_attn_kernel, tq=tq),
        grid=(B, H // hb, S // tq),
        in_specs=[
            pl.BlockSpec((1, hb, tq, MLA_QK_PAD), lambda b, h, i: (b, h, i, 0)),
            pl.BlockSpec((1, hb, S, MLA_QK_PAD), lambda b, h, i: (b, h, 0, 0)),
            pl.BlockSpec((1, hb, S // tq, MLA_V, tq), lambda b, h, i: (b, h, 0, 0, 0)),
        ],
        out_specs=pl.BlockSpec((1, tq, hb * MLA_V), lambda b, h, i: (b, i, h)),
        out_shape=jax.ShapeDtypeStruct((B, S, H * MLA_V), F32),
        compiler_params=pltpu.CompilerParams(dimension_semantics=("parallel", "parallel", "arbitrary"),
                                             vmem_limit_bytes=VMEM_LIMIT),
        name="mla_attn",
    )(qm, km, vm)


def _t5_bucket(dist):
    n = jnp.maximum(dist, 0)
    max_exact = N_BUCKETS // 2
    nf = jnp.maximum(n, 1).astype(F32)
    large = max_exact + (jnp.log(nf / max_exact) / math.log(MAX_DISTANCE / max_exact)
                         * (N_BUCKETS - max_exact)).astype(jnp.int32)
    large = jnp.minimum(large, N_BUCKETS - 1)
    return jnp.where(n < max_exact, n, large)


def _swa_attn_kernel(sink_ref, q_ref, kp_ref, kc_ref, vp_ref, vc_ref, pcp_ref, pcc_ref, prow_ref, tab_ref, o_ref):
    n = pl.program_id(1)
    band = 2 * BLOCK
    kb = jnp.concatenate([kp_ref[0], kc_ref[0]], axis=0)
    vb = jnp.concatenate([vp_ref[0], vc_ref[0]], axis=0)
    kpos = jnp.concatenate([pcp_ref[0], pcc_ref[0]], axis=0)
    bucket = _t5_bucket(prow_ref[0] - kpos)

    key = lax.broadcasted_iota(jnp.int32, (band, BLOCK), 0)
    qry = lax.broadcasted_iota(jnp.int32, (band, BLOCK), 1)
    off = qry + BLOCK - key
    valid = (off >= 0) & (off < WINDOW) & ((key >= BLOCK) | (n > 0))

    lo_lane = lax.broadcasted_iota(jnp.int32, (1, LANES), 1) < SWA_HD
    lo_row = lax.broadcasted_iota(jnp.int32, (LANES, 1), 0) < SWA_HD
    zero = jnp.zeros((), BF16)

    pairs = SWA_GROUP // 2
    scores = {}
    for g in range(SWA_KV_HEADS):
        qg = jnp.concatenate([q_ref[0, :, (g * pairs + pi) * LANES:(g * pairs + pi + 1) * LANES]
                              for pi in range(pairs)], axis=0)
        kg = kb[:, g * LANES:(g + 1) * LANES]
        scores[g, 0] = _dot_nt(jnp.where(lo_lane, kg, zero), qg)
        scores[g, 1] = _dot_nt(jnp.where(lo_lane, zero, kg), qg)

    bias = []
    for h in range(SWA_HEADS):
        tab_row = jnp.broadcast_to(tab_ref[h:h + 1, :], (band, LANES))
        bias.append(jnp.take_along_axis(tab_row, bucket, axis=1))

    outs = []
    for g in range(SWA_KV_HEADS):
        probs = ([], [])
        inv = ([], [])
        for pi in range(pairs):
            for half in range(2):
                h = 2 * (g * pairs + pi) + half
                s = scores[g, half][:, pi * BLOCK:(pi + 1) * BLOCK]
                s = jnp.where(valid, s + bias[h], NEG_INF)
                sink = sink_ref[h]
                m = jnp.maximum(jnp.max(s, axis=0, keepdims=True), sink)
                p = jnp.exp(s - m)
                inv[half].append(1.0 / (jnp.sum(p, axis=0, keepdims=True) + jnp.exp(sink - m)))
                probs[half].append(p.astype(BF16))
        vt = vb[:, g * LANES:(g + 1) * LANES].astype(F32).T.astype(BF16)
        ot = (_dot(jnp.where(lo_row, vt, zero), jnp.concatenate(probs[0], axis=1)) * jnp.concatenate(inv[0], axis=1)
              + _dot(jnp.where(lo_row, zero, vt), jnp.concatenate(probs[1], axis=1)) * jnp.concatenate(inv[1], axis=1))
        for pi in range(pairs):
            outs.append(ot[:, pi * BLOCK:(pi + 1) * BLOCK].T)
    o_ref[0] = jnp.concatenate(outs, axis=1)


def _swa_attn(sinks, qs, ks, vs, pos_col, pos_row, tab):
    B, S, _ = qs.shape
    nb = S // BLOCK
    cur = lambda w: pl.BlockSpec((1, BLOCK, w), lambda b, n: (b, n, 0))
    prev = lambda w: pl.BlockSpec((1, BLOCK, w), lambda b, n: (b, jnp.maximum(n - 1, 0), 0))
    return pl.pallas_call(
        _swa_attn_kernel,
        grid=(B, nb),
        in_specs=[
            pl.BlockSpec(memory_space=pltpu.SMEM),
            cur(SWA_HEADS * SWA_HD), prev(2 * LANES), cur(2 * LANES), prev(2 * LANES), cur(2 * LANES),
            prev(1), cur(1),
            pl.BlockSpec((1, 1, BLOCK), lambda b, n: (b, 0, n)),
            _resident(tab.shape),
        ],
        out_specs=cur(SWA_HEADS * SWA_HD),
        out_shape=jax.ShapeDtypeStruct((B, S, SWA_HEADS * SWA_HD), F32),
        compiler_params=pltpu.CompilerParams(dimension_semantics=("parallel", "arbitrary"), vmem_limit_bytes=VMEM_LIMIT),
        name="swa_attn",
    )(sinks, qs, ks, ks, vs, vs, pos_col, pos_col, pos_row, tab)


def _out_proj_kernel(om_ref, os_ref, x_ref, gout_ref, wout_ref, g2_ref, h_ref, n2_ref):
    om = om_ref[...]
    osw = os_ref[...]
    half = om.shape[-1]
    a = om * _rms_scale(om, half) * gout_ref[:, :half]
    b = osw * _rms_scale(osw, half) * gout_ref[:, half:]
    mixed = jnp.concatenate([a.astype(BF16), b.astype(BF16)], axis=1)
    h = x_ref[...] + _dot(mixed, wout_ref[...])
    h_ref[...] = h
    n2_ref[...] = (h * _rms_scale(h, h.shape[-1]) * g2_ref[...]).astype(BF16)


def _out_proj(om, osw, x2, gout, wout, g2, *, tm):
    T, D = x2.shape
    half = om.shape[-1]
    tok = lambda w: pl.BlockSpec((tm, w), lambda t: (t, 0))
    return pl.pallas_call(
        _out_proj_kernel,
        grid=(T // tm,),
        in_specs=[tok(half), tok(half), tok(D), _resident(gout.shape), _resident(wout.shape), _resident(g2.shape)],
        out_specs=[tok(D), tok(D)],
        out_shape=[jax.ShapeDtypeStruct((T, D), F32), jax.ShapeDtypeStruct((T, D), BF16)],
        compiler_params=pltpu.CompilerParams(dimension_semantics=("parallel",), vmem_limit_bytes=VMEM_LIMIT),
        name="out_proj",
    )(om, osw, x2, gout, wout, g2)


_CAND = [(a, b) for a in range(PEER_TOPK) for b in range(PEER_TOPK) if (a + 1) * (b + 1) <= PEER_TOPK]


def _oddeven_merge(lo, hi, r):
    step = r * 2
    if step < hi - lo:
        yield from _oddeven_merge(lo, hi, step)
        yield from _oddeven_merge(lo + r, hi, step)
        yield from [(i, i + r) for i in range(lo + r, hi - r, step)]
    else:
        yield (lo, lo + r)


def _oddeven_merge_sort(lo, hi):
    if hi - lo >= 1:
        mid = lo + (hi - lo) // 2
        yield from _oddeven_merge_sort(lo, mid)
        yield from _oddeven_merge_sort(mid + 1, hi)
        yield from _oddeven_merge(lo, hi, 1)


_SORT16 = tuple(_oddeven_merge_sort(0, PEER_TOPK - 1))
_BITONIC16 = tuple((i, i + s) for s in (8, 4, 2, 1) for i in range(PEER_TOPK) if not i & s)


def _peer_route_kernel(n2_ref, wqt_ref, sk_ref, cnt_ref, a_ref, r2_ref, b_ref,
                       s_scr, rank_scr, vals_scr, n_scr, m_scr, z_scr, tie_scr, *, tr):
    n_lt = tr // LANES
    n_hp = 2 * PEER_HEADS
    qt = lax.dot_general(wqt_ref[...], n2_ref[...], (((1,), (1,)), ((), ())),
                         preferred_element_type=F32).astype(BF16)
    for hp in range(n_hp):
        s_scr[hp] = _dot(sk_ref[hp], qt[hp * PEER_HALF:(hp + 1) * PEER_HALF, :])

    key = lax.broadcasted_iota(jnp.int32, (PEER_NKEYS, LANES), 0).astype(F32)
    head = lax.broadcasted_iota(jnp.int32, (PEER_HEADS, LANES), 0)
    vals_scr[...] = jnp.zeros_like(vals_scr)

    sub = 8
    n_vr = PEER_NKEYS // sub

    def exchange(vals, i, j):
        vals[i], vals[j] = jnp.maximum(vals[i], vals[j]), jnp.minimum(vals[i], vals[j])

    def top16_network(hp, any_tie):
        h = hp // 2
        p = hp % 2
        tie = jnp.zeros((sub, LANES), F32)
        for lt in range(n_lt):
            lanes = slice(lt * LANES, (lt + 1) * LANES)
            x = [s_scr[hp, vi * sub:(vi + 1) * sub, lanes] for vi in range(n_vr)]
            top = list(x)
            for i, j in _SORT16:
                exchange(top, i, j)
            for shift in (4, 2, 1):
                other = [pltpu.roll(top[PEER_TOPK - 1 - r], shift, axis=0) for r in range(PEER_TOPK)]
                top = [jnp.maximum(top[r], other[r]) for r in range(PEER_TOPK)]
                for i, j in _BITONIC16:
                    exchange(top, i, j)
            for r in range(PEER_TOPK):
                vals_scr[p, r, :, lanes] = jnp.where(head == h, top[r], vals_scr[p, r, :, lanes])
                if r:
                    tie = jnp.where(top[r - 1] == top[r], 1.0, tie)
            inside = None
            for vi in range(n_vr):
                rank = jnp.zeros((sub, LANES), F32)
                for r in range(PEER_TOPK):
                    rank = jnp.where(top[r] > x[vi], float(r + 1), rank)
                rank_scr[hp, vi * sub:(vi + 1) * sub, lanes] = rank
                ins = jnp.where(x[vi] >= top[PEER_TOPK - 1], 1.0, 0.0)
                inside = ins if inside is None else inside + ins
            for shift in (4, 2, 1):
                inside = inside + pltpu.roll(inside, shift, axis=0)
            tie = jnp.where(inside != float(PEER_TOPK), 1.0, tie)
        tie_scr[hp] = tie
        return jnp.maximum(any_tie, tie)

    any_tie = lax.fori_loop(0, n_hp, top16_network, jnp.zeros((sub, LANES), F32))

    def top16(hp, _):
        h = hp // 2
        p = hp % 2

        @pl.when(jnp.max(tie_scr[hp]) > 0.0)
        def _():
            for lt in range(n_lt):
                lanes = slice(lt * LANES, (lt + 1) * LANES)
                v = s_scr[hp, :, lanes]
                rank = jnp.full((PEER_NKEYS, LANES), float(PEER_TOPK), F32)
                for r in range(PEER_TOPK):
                    m = jnp.max(v, axis=0, keepdims=True)
                    first = jnp.min(jnp.where(v == m, key, float(PEER_NKEYS)), axis=0, keepdims=True)
                    hit = key == first
                    v = jnp.where(hit, -jnp.inf, v)
                    rank = jnp.where(hit, float(r), rank)
                    vals_scr[p, r, :, lanes] = jnp.where(head == h, m, vals_scr[p, r, :, lanes])
                rank_scr[hp, :, lanes] = rank
        return 0

    @pl.when(jnp.max(any_tie) > 0.0)
    def _():
        lax.fori_loop(0, n_hp, top16, 0)

    for lt in range(n_lt):
        lanes = slice(lt * LANES, (lt + 1) * LANES)
        v1 = [vals_scr[0, a, :, lanes] for a in range(PEER_TOPK)]
        v2 = [vals_scr[1, b, :, lanes] for b in range(PEER_TOPK)]
        c = [v1[a] + v2[b] for (a, b) in _CAND]
        flat = [float(a * PEER_TOPK + b) for (a, b) in _CAND]
        for _ in range(PEER_TOPK):
            m = functools.reduce(jnp.maximum, c)
            first = functools.reduce(jnp.minimum, [jnp.where(ci == m, fi, 1e9) for ci, fi in zip(c, flat)])
            c = [jnp.where(first == fi, -jnp.inf, ci) for ci, fi in zip(c, flat)]
        e1 = [jnp.exp(v1[a] - v1[0]) for a in range(PEER_TOPK)]
        e2 = [jnp.exp(v2[b] - v2[0]) for b in range(PEER_TOPK)]
        z = jnp.zeros_like(v1[0])
        n_a = [jnp.zeros_like(v1[0]) for _ in range(PEER_TOPK)]
        for ci, (a, b) in zip(c, _CAND):
            taken = ci == -jnp.inf
            z = z + jnp.where(taken, e1[a] * e2[b], 0.0)
            n_a[a] = n_a[a] + jnp.where(taken, 1.0, 0.0)
        for a in range(PEER_TOPK):
            n_scr[a, :, lanes] = n_a[a]
        m_scr[0, :, lanes] = v1[0]
        m_scr[1, :, lanes] = v2[0]
        z_scr[:, lanes] = 1.0 / z

    def spread(h, _):
        rank1 = rank_scr[2 * h]
        cnt = jnp.zeros((PEER_NKEYS, tr), F32)
        for a in range(PEER_TOPK):
            cnt = jnp.where(rank1 == float(a), n_scr[a, pl.ds(h, 1), :], cnt)
        a = jnp.exp(s_scr[2 * h] - m_scr[0, pl.ds(h, 1), :]) * z_scr[pl.ds(h, 1), :]
        for lt in range(n_lt):
            cnt_ref[h, lt] = cnt[:, lt * LANES:(lt + 1) * LANES]
            a_ref[h, lt] = a[:, lt * LANES:(lt + 1) * LANES]
        r2_ref[h] = rank_scr[2 * h + 1].astype(BF16)
        b_ref[h] = jnp.exp(s_scr[2 * h + 1] - m_scr[1, pl.ds(h, 1), :]).astype(BF16)
        return 0

    lax.fori_loop(0, PEER_HEADS, spread, 0)


def _peer_route(n2, wqt, sk, *, tr):
    T, D = n2.shape
    route = pl.BlockSpec((PEER_HEADS, PEER_NKEYS, tr), lambda t: (0, 0, t))
    shape = lambda dt: jax.ShapeDtypeStruct((PEER_HEADS, PEER_NKEYS, T), dt)
    slab = pl.BlockSpec((PEER_HEADS, tr // LANES, PEER_NKEYS, LANES), lambda t: (0, t, 0, 0))
    slab_shape = jax.ShapeDtypeStruct((PEER_HEADS, T // LANES, PEER_NKEYS, LANES), F32)
    return pl.pallas_call(
        functools.partial(_peer_route_kernel, tr=tr),
        grid=(T // tr,),
        in_specs=[pl.BlockSpec((tr, D), lambda t: (t, 0)), _resident(wqt.shape), _resident(sk.shape)],
        out_specs=[slab, slab, route, route],
        out_shape=[slab_shape, slab_shape, shape(BF16), shape(BF16)],
        scratch_shapes=[
            pltpu.VMEM((2 * PEER_HEADS, PEER_NKEYS, tr), F32),
            pltpu.VMEM((2 * PEER_HEADS, PEER_NKEYS, tr), F32),
            pltpu.VMEM((2, PEER_TOPK, PEER_HEADS, tr), F32),
            pltpu.VMEM((PEER_TOPK, PEER_HEADS, tr), F32),
            pltpu.VMEM((2, PEER_HEADS, tr), F32),
            pltpu.VMEM((PEER_HEADS, tr), F32),
            pltpu.VMEM((2 * PEER_HEADS, 8, LANES), F32),
        ],
        compiler_params=pltpu.CompilerParams(dimension_semantics=("parallel",), vmem_limit_bytes=VMEM_LIMIT),
        name="peer_route",
    )(n2, wqt, sk)


PACK = 16


def _peer_dense_kernel(n2_ref, u_ref, vt_ref, cnt_ref, a_ref, r2_ref, b_ref, h_ref, o_ref, acc_ref, act_scr, w_scr, *, te):
    e = pl.program_id(1)
    tm = n2_ref.shape[0]
    n_sub = te // PEER_NKEYS

    last = pl.num_programs(1) - 1
    cur = e % 2
    zero = jnp.zeros((), BF16)

    def front_matmul():
        act_scr[...] = _dot_nt(u_ref[...], n2_ref[...])

    def back_matmul():
        return _dot(vt_ref[...], w_scr[1 - cur])

    def gate_tile():
        for ii in range(n_sub):
            i = e * n_sub + ii
            row16 = lambda ref, h: jnp.concatenate(
                [ref[h, lt, pl.ds(i, PACK, stride=0), :] for lt in range(tm // LANES)], axis=1).astype(BF16)
            cnt = [row16(cnt_ref, h) for h in range(PEER_HEADS)]
            a = [row16(a_ref, h) for h in range(PEER_HEADS)]
            for jg in range(PEER_NKEYS // PACK):
                keys = slice(jg * PACK, (jg + 1) * PACK)
                gate = None
                for h in range(PEER_HEADS):
                    term = jnp.where(r2_ref[h, keys, :] < cnt[h], b_ref[h, keys, :], zero) * a[h]
                    gate = term if gate is None else gate + term
                rows = slice(ii * PEER_NKEYS + jg * PACK, ii * PEER_NKEYS + (jg + 1) * PACK)
                w_scr[cur, rows, :] = jax.nn.gelu(act_scr[rows, :]).astype(BF16) * gate

    @pl.when(e == 0)
    def _():
        front_matmul()
        acc_ref[...] = jnp.zeros_like(acc_ref)
        gate_tile()

    @pl.when((e > 0) & (e < last))
    def _():
        front_matmul()
        acc_ref[...] += back_matmul()
        gate_tile()

    @pl.when(e == last)
    def _():
        o_ref[...] = h_ref[...] + (acc_ref[...] + back_matmul()).T


def _peer_dense(n2, u, vt, cnt, a, r2, b, h, *, tm, te):
    T, D = n2.shape
    n_tiles = u.shape[0] // te
    route = pl.BlockSpec((PEER_HEADS, PEER_NKEYS, tm), lambda t, e: (0, 0, t))
    route_once = pl.BlockSpec((PEER_HEADS, tm // LANES, PEER_NKEYS, LANES), lambda t, e: (0, t, 0, 0),
                              pipeline_mode=pl.Buffered(1))
    return pl.pallas_call(
        functools.partial(_peer_dense_kernel, te=te),
        grid=(T // tm, n_tiles + 1),
        in_specs=[
            pl.BlockSpec((tm, D), lambda t, e: (t, 0), pipeline_mode=pl.Buffered(1)),
            pl.BlockSpec((te, D), lambda t, e: (jnp.minimum(e, n_tiles - 1), 0)),
            pl.BlockSpec((D, te), lambda t, e: (0, jnp.maximum(e - 1, 0))),
            route_once, route_once, route, route,
            pl.BlockSpec((tm, D), lambda t, e: (t, 0)),
        ],
        out_specs=pl.BlockSpec((tm, D), lambda t, e: (t, 0)),
        out_shape=jax.ShapeDtypeStruct((T, D), F32),
        scratch_shapes=[pltpu.VMEM((D, tm), F32), pltpu.VMEM((te, tm), F32), pltpu.VMEM((2, te, tm), BF16)],
        compiler_params=pltpu.CompilerParams(dimension_semantics=("parallel", "arbitrary"), vmem_limit_bytes=VMEM_LIMIT),
        name="peer_dense",
    )(n2, u, vt, cnt, a, r2, b, h)


def _row(v, width=None):
    v = v.astype(F32).reshape(1, -1)
    if width is not None and v.shape[1] < width:
        v = jnp.pad(v, ((0, 0), (0, width - v.shape[1])))
    return v


def _rope_consts():
    half = MLA_ROPE // 2
    inv_freq = ROPE_THETA ** (-jnp.arange(half, dtype=F32) / half)
    z = jnp.zeros((LANES - MLA_ROPE,), F32)
    rows = [
        jnp.concatenate([inv_freq, inv_freq, z]),
        jnp.concatenate([jnp.ones((MLA_ROPE,), F32), z]),
        jnp.concatenate([-jnp.ones((half,), F32), jnp.ones((half,), F32), z]),
    ]
    return jnp.pad(jnp.stack(rows), ((0, 5), (0, 0)))


def _layer(x, positions, norm1_gain, w_in, q_a_gain, w_q_b, kv_a_gain, w_kv_b, mla_q_gain, mla_k_gain,
           swa_q_gain, swa_k_gain, swa_sinks, rel_bias_table, group_out_gain, w_out, norm2_gain,
           peer_w_q, peer_sub_keys, peer_u, peer_v, *, tm_in, tq, tm_out, tr, tm_peer, te):
    B, S, D = x.shape
    T = B * S

    zpad = jnp.zeros((D, LANES - MLA_ROPE), w_in.dtype)
    o = np.cumsum((MLA_Q_RANK, MLA_KV_RANK, MLA_ROPE, SWA_HEADS * SWA_HD, SWA_KV_HEADS * SWA_HD)).tolist()
    k_swa, v_swa = w_in[:, o[3]:o[4]], w_in[:, o[4]:]
    dup = lambda w: jnp.concatenate([w[:, :SWA_HD], w[:, :SWA_HD], w[:, SWA_HD:], w[:, SWA_HD:]], axis=1)
    win = jnp.concatenate([w_in[:, :o[2]], zpad, w_in[:, o[2]:o[3]], dup(k_swa), dup(v_swa)], axis=1).astype(BF16)
    wqb = jnp.pad(w_q_b.reshape(MLA_Q_RANK, MLA_HEADS, MLA_QK), ((0, 0), (0, 0), (0, MLA_QK_PAD - MLA_QK)))
    wqb = wqb.reshape(MLA_Q_RANK, MLA_HEADS * MLA_QK_PAD).astype(BF16)

    wkv = w_kv_b.reshape(MLA_KV_RANK, MLA_HEADS, MLA_NOPE + MLA_V)
    wkn = wkv[:, :, :MLA_NOPE].reshape(MLA_KV_RANK, MLA_HEADS * MLA_NOPE).astype(BF16)
    wvt = wkv[:, :, MLA_NOPE:].transpose(1, 2, 0).astype(BF16)

    qm, km, vm, qs, ks, vs = _in_proj(
        x, positions.reshape(B, S, 1), _row(norm1_gain), win, _row(q_a_gain), wqb, _row(kv_a_gain),
        wkn, wvt, _row(mla_q_gain, MLA_QK_PAD), _row(mla_k_gain, MLA_QK_PAD),
        _row(jnp.tile(swa_q_gain, 2)), _row(jnp.tile(swa_k_gain, 2)), _rope_consts(), tm=tm_in, tk=tq)

    o_mla = _mla_attn(qm, km, vm, tq=tq, hb=4)
    tab = jnp.pad(rel_bias_table.astype(F32).T, ((0, 0), (0, LANES - N_BUCKETS)))
    o_swa = _swa_attn(swa_sinks.astype(F32), qs, ks, vs, positions.reshape(B, S, 1), positions.reshape(B, 1, S), tab)

    h, n2 = _out_proj(o_mla.reshape(T, -1), o_swa.reshape(T, -1), x.reshape(T, D), _row(group_out_gain),
                      w_out.astype(BF16), _row(norm2_gain), tm=tm_out)

    sk = peer_sub_keys.reshape(2 * PEER_HEADS, PEER_NKEYS, PEER_HALF).astype(BF16)
    cnt, a, r2, b = _peer_route(n2, peer_w_q.T.astype(BF16), sk, tr=tr)
    out = _peer_dense(n2, peer_u.astype(BF16), peer_v.T.astype(BF16), cnt, a, r2, b, h, tm=tm_peer, te=te)
    return out.reshape(B, S, D)


def kernel(x, positions, norm1_gain, w_in, q_a_gain, w_q_b, kv_a_gain, w_kv_b, mla_q_gain, mla_k_gain, swa_q_gain, swa_k_gain, swa_sinks, rel_bias_table, group_out_gain, w_out, norm2_gain, peer_w_q, peer_sub_keys, peer_u, peer_v):
    assert norm1_gain.shape[0] == 1, "single-layer trunk"
    return _layer(x, positions, norm1_gain[0], w_in[0], q_a_gain[0], w_q_b[0], kv_a_gain[0], w_kv_b[0],
                  mla_q_gain[0], mla_k_gain[0], swa_q_gain[0], swa_k_gain[0], swa_sinks[0], rel_bias_table,
                  group_out_gain[0], w_out[0], norm2_gain[0], peer_w_q[0], peer_sub_keys[0], peer_u[0], peer_v[0],
                  tm_in=512, tq=512, tm_out=512, tr=512, tm_peer=512, te=1024)
```
